```python
import math
import jax, jax.numpy as jnp
from jax import lax
import numpy as np

D_MODEL = 1024
BATCH = 8
SEQ = 8192
DEPTH = 1

CHUNK = 64
D_MIX = D_MODEL
D_GMLP = D_MIX // 2
GMLP_GROUPS = 4
GMLP_GROUP_W = D_GMLP // GMLP_GROUPS
SGU_BLOCK = 128
D_DIFF = D_MIX - D_GMLP
DIFF_HEADS = 4
DIFF_QK_DIM = D_DIFF // DIFF_HEADS // 2
DIFF_V_DIM = 2 * DIFF_QK_DIM
Q_BLOCK = 128
REL_BUCKETS = 32
REL_MAX_DIST = 128
N_EXPERTS = 32
TOP_K = 4
D_EXPERT = D_MODEL
SWIGLU_ALPHA = 1.702
SWIGLU_LIMIT = 7.0
MOE_BLOCK = 256
NORM_EPS = 1e-5
NEG_INF = -1e30
D_IN = 2 * D_GMLP + 2 * D_DIFF + DIFF_HEADS * DIFF_V_DIM

kernel_name = 'hybrid_gmlp_diffattn_moe'

F32 = jnp.float32


def rms_norm(x, g):
    xf = x.astype(F32)
    y = xf * lax.rsqrt(jnp.mean(xf * xf, axis=-1, keepdims=True) + NORM_EPS)
    return (y * g.astype(F32)).astype(x.dtype)


def layer_norm(x, g, b):
    xf = x.astype(F32)
    mu = jnp.mean(xf, axis=-1, keepdims=True)
    xc = xf - mu
    y = xc * lax.rsqrt(jnp.mean(xc * xc, axis=-1, keepdims=True) + NORM_EPS)
    return (y * g.astype(F32) + b.astype(F32)).astype(x.dtype)


def rel_bucket(rel):
    nb = REL_BUCKETS // 2
    max_exact = nb // 2
    side = jnp.where(rel > 0, nb, 0)
    n = jnp.abs(rel)
    nf = jnp.maximum(n, 1).astype(F32)
    large = max_exact + (jnp.log(nf / max_exact) / math.log(REL_MAX_DIST / max_exact)
                         * (nb - max_exact)).astype(jnp.int32)
    large = jnp.minimum(large, nb - 1)
    return side + jnp.where(n < max_exact, n, large)


def gmlp_sgu(z, ln_g, ln_b, w_s, b_s):
    B, S = z.shape[0], z.shape[1]
    u, v = jnp.split(z, 2, axis=-1)
    v = layer_norm(v, ln_g, ln_b)
    n_blk = S // SGU_BLOCK
    v = v.reshape(B, n_blk, SGU_BLOCK, GMLP_GROUPS, GMLP_GROUP_W)
    pos = jnp.arange(SGU_BLOCK)
    mask = (pos[None, :] // CHUNK) <= (pos[:, None] // CHUNK)
    w = jnp.where(mask[None], w_s, 0)
    sv = jnp.einsum('gij,bnjgc->bnigc', w, v) + b_s.T[None, None, :, :, None]
    return u * sv.reshape(B, S, D_GMLP)


def diff_attention(q, k, v, rel_table, lam, subln_g, lam_init):
    B, S = q.shape[0], q.shape[1]
    n_blk = S // Q_BLOCK
    scale = DIFF_QK_DIM ** -0.5
    qb = q.reshape(B, n_blk, Q_BLOCK, DIFF_HEADS, 2, DIFF_QK_DIM).transpose(1, 0, 3, 4, 2, 5)
    kt = k.transpose(0, 2, 3, 1, 4)
    vt = v.transpose(0, 2, 1, 3)
    kpos = jnp.arange(S)

    def block(args):
        qblk, bi = args
        qpos = bi * Q_BLOCK + jnp.arange(Q_BLOCK)
        rel = kpos[None, :] - qpos[:, None]
        bias = rel_table[rel_bucket(rel)].astype(F32).transpose(2, 0, 1)
        visible = (kpos[None, :] // CHUNK) <= (qpos[:, None] // CHUNK)
        logits = jnp.einsum('bhmqd,bhmkd->bhmqk', qblk, kt).astype(F32) * scale + bias[None, :, None]
        logits = jnp.where(visible, logits, NEG_INF)
        p = jax.nn.softmax(logits, axis=-1)
        attn = p[:, :, 0] - lam * p[:, :, 1]
        return jnp.einsum('bhqk,bhkd->bhqd', attn.astype(v.dtype), vt)

    o = lax.map(block, (qb, jnp.arange(n_blk)))
    o = o.transpose(1, 0, 3, 2, 4).reshape(B, S, DIFF_HEADS, DIFF_V_DIM)
    o = rms_norm(o, subln_g) * (1.0 - lam_init)
    return o.reshape(B, S, D_DIFF)


def moe(h, router_w, router_b, w1, b1, w2, b2):
    B, S, D = h.shape
    n = B * S
    hf = h.reshape(n, D)
    logits = (hf @ router_w + router_b).astype(F32)
    top_v, top_e = lax.top_k(logits, TOP_K)
    gate = jax.nn.softmax(top_v, axis=-1).astype(h.dtype)
    nk = n * TOP_K
    e_flat = top_e.reshape(nk).astype(jnp.int32)
    tok_flat = jnp.arange(nk, dtype=jnp.int32) // TOP_K
    g_flat = gate.reshape(nk)
    order = jnp.argsort(e_flat)
    e_sorted = e_flat[order]
    counts = jax.ops.segment_sum(jnp.ones((nk,), jnp.int32), e_flat, num_segments=N_EXPERTS)
    starts = jnp.cumsum(counts) - counts
    padded = (counts + MOE_BLOCK - 1) // MOE_BLOCK * MOE_BLOCK
    pad_end = jnp.cumsum(padded)
    pad_start = pad_end - padded
    dest = pad_start[e_sorted] + jnp.arange(nk, dtype=jnp.int32) - starts[e_sorted]
    n_blocks = -(-nk // MOE_BLOCK) + N_EXPERTS
    cap = n_blocks * MOE_BLOCK
    row_tok = jnp.zeros((cap,), jnp.int32).at[dest].set(tok_flat[order])
    row_gate = jnp.zeros((cap,), h.dtype).at[dest].set(g_flat[order])
    block_start = jnp.arange(n_blocks, dtype=jnp.int32) * MOE_BLOCK
    block_exp = jnp.minimum(jnp.sum(pad_end[None, :] <= block_start[:, None], axis=1),
                            N_EXPERTS - 1).astype(jnp.int32)

    def expert_block(args):
        tok, gt, e = args
        a = hf[tok] @ w1[e] + b1[e]
        a_glu = jnp.minimum(a[:, ::2], SWIGLU_LIMIT)
        a_lin = jnp.clip(a[:, 1::2], -SWIGLU_LIMIT, SWIGLU_LIMIT)
        act = a_glu * jax.nn.sigmoid(SWIGLU_ALPHA * a_glu) * (a_lin + 1)
        return (act @ w2[e] + b2[e]) * gt[:, None]

    out = lax.map(expert_block, (row_tok.reshape(n_blocks, MOE_BLOCK),
                                 row_gate.reshape(n_blocks, MOE_BLOCK), block_exp))
    y = jnp.zeros((n, D), h.dtype).at[row_tok].add(out.reshape(cap, D))
    return y.reshape(B, S, D)


def setup_inputs(seed: int = 0) -> dict:
    key = jax.random.key(seed)
    ks = jax.random.split(key, 24)
    nrm = lambda k, shape: jax.random.normal(k, shape, F32)
    L = DEPTH
    return {
        'x': nrm(ks[0], (BATCH, SEQ, D_MODEL)),
        'rel_bias_table': 0.5 * nrm(ks[1], (REL_BUCKETS, DIFF_HEADS)),
        'mix_norm_g': 1.0 + 0.02 * nrm(ks[2], (L, D_MODEL)),
        'w_in': nrm(ks[3], (L, D_MODEL, D_IN)) * D_MODEL ** -0.5,
        'sgu_ln_g': 1.0 + 0.02 * nrm(ks[4], (L, D_GMLP)),
        'sgu_ln_b': 0.02 * nrm(ks[5], (L, D_GMLP)),
        'sgu_w': nrm(ks[6], (L, GMLP_GROUPS, SGU_BLOCK, SGU_BLOCK)) * SGU_BLOCK ** -0.5,
        'sgu_b': 1.0 + 0.1 * nrm(ks[7], (L, GMLP_GROUPS, SGU_BLOCK)),
        'gmlp_out_g': 1.0 + 0.02 * nrm(ks[8], (L, D_GMLP)),
        'lambda_q1': 0.1 * nrm(ks[9], (L, DIFF_QK_DIM)),
        'lambda_k1': 0.1 * nrm(ks[10], (L, DIFF_QK_DIM)),
        'lambda_q2': 0.1 * nrm(ks[11], (L, DIFF_QK_DIM)),
        'lambda_k2': 0.1 * nrm(ks[12], (L, DIFF_QK_DIM)),
        'diff_subln_g': 1.0 + 0.02 * nrm(ks[13], (L, DIFF_V_DIM)),
        'w_out': nrm(ks[14], (L, D_MIX, D_MODEL)) * D_MIX ** -0.5,
        'ffn_norm_g': 1.0 + 0.02 * nrm(ks[15], (L, D_MODEL)),
        'router_w': nrm(ks[16], (L, D_MODEL, N_EXPERTS)) * D_MODEL ** -0.5,
        'router_b': 0.01 * nrm(ks[17], (L, N_EXPERTS)),
        'w_mlp1': nrm(ks[18], (L, N_EXPERTS, D_MODEL, 2 * D_EXPERT)) * D_MODEL ** -0.5,
        'b_mlp1': 0.01 * nrm(ks[19], (L, N_EXPERTS, 2 * D_EXPERT)),
        'w_mlp2': nrm(ks[20], (L, N_EXPERTS, D_EXPERT, D_MODEL)) * D_EXPERT ** -0.5,
        'b_mlp2': 0.01 * nrm(ks[21], (L, N_EXPERTS, D_MODEL)),
        'final_norm_g': 1.0 + 0.02 * nrm(ks[22], (D_MODEL,)),
    }


def reference(x, rel_bias_table, mix_norm_g, w_in, sgu_ln_g, sgu_ln_b, sgu_w, sgu_b, gmlp_out_g,
              lambda_q1, lambda_k1, lambda_q2, lambda_k2, diff_subln_g, w_out, ffn_norm_g,
              router_w, router_b, w_mlp1, b_mlp1, w_mlp2, b_mlp2, final_norm_g):
    B, S = x.shape[0], x.shape[1]
    for l in range(DEPTH):
        h = rms_norm(x, mix_norm_g[l])
        proj = h @ w_in[l]
        z, q, k, v = jnp.split(proj, [2 * D_GMLP, 2 * D_GMLP + D_DIFF, 2 * D_GMLP + 2 * D_DIFF], axis=-1)
        g_out = gmlp_sgu(jax.nn.gelu(z, approximate=False), sgu_ln_g[l], sgu_ln_b[l], sgu_w[l], sgu_b[l])
        g_out = rms_norm(g_out, gmlp_out_g[l])
        lam_init = 0.8 - 0.6 * math.exp(-0.3 * l)
        lam = (jnp.exp(jnp.sum(lambda_q1[l].astype(F32) * lambda_k1[l].astype(F32)))
               - jnp.exp(jnp.sum(lambda_q2[l].astype(F32) * lambda_k2[l].astype(F32))) + lam_init)
        d_out = diff_attention(q.reshape(B, S, DIFF_HEADS, 2, DIFF_QK_DIM),
                               k.reshape(B, S, DIFF_HEADS, 2, DIFF_QK_DIM),
                               v.reshape(B, S, DIFF_HEADS, DIFF_V_DIM),
                               rel_bias_table, lam, diff_subln_g[l], lam_init)
        x = x + jnp.concatenate([g_out, d_out], axis=-1) @ w_out[l]
        x = x + moe(rms_norm(x, ffn_norm_g[l]), router_w[l], router_b[l],
                    w_mlp1[l], b_mlp1[l], w_mlp2[l], b_mlp2[l])
    return rms_norm(x, final_norm_g)
```

```python
import functools
import math

import jax
import jax.numpy as jnp
from jax import lax
from jax.experimental import pallas as pl
from jax.experimental.pallas import tpu as pltpu

F32 = jnp.float32
BF16 = jnp.bfloat16

CHUNK = 64
GMLP_GROUPS = 4
SGU_BLOCK = 128
DIFF_HEADS = 4
DIFF_QK_DIM = 64
DIFF_V_DIM = 128
REL_BUCKETS = 32
REL_MAX_DIST = 128
N_EXPERTS = 32
TOP_K = 4
SWIGLU_ALPHA = 1.702
SWIGLU_LIMIT = 7.0
NORM_EPS = 1e-5
NEG_INF = -1e30

LANES = 128
ATT_TILE = 256
TOKEN_TILE = 512
MOE_BLOCK = 256
ROW_TILE = 256
VMEM_LIMIT = 56 * 1024 * 1024


def _rms(x, g):
    return x * lax.rsqrt(jnp.mean(x * x, axis=-1, keepdims=True) + NORM_EPS) * g


def _gelu(z):
    return 0.5 * z * (1.0 + lax.erf(z * (1.0 / math.sqrt(2.0))))


def _mix_in_kernel(x_ref, ng_ref, win_ref, lng_ref, lnb_ref, sw_ref, sb_ref, og_ref,
                   gout_ref, qt_ref, k_ref, vt_ref):
    tm = x_ref.shape[0]
    dg = gout_ref.shape[1]
    gw = dg // GMLP_GROUPS
    hb = _rms(x_ref[...], ng_ref[...]).astype(BF16)

    z = _gelu(jnp.dot(hb, win_ref[:, :2 * dg], preferred_element_type=F32))
    u = z[:, :dg]
    v = z[:, dg:]
    mu = jnp.mean(v, axis=-1, keepdims=True)
    vc = v - mu
    v = vc * lax.rsqrt(jnp.mean(vc * vc, axis=-1, keepdims=True) + NORM_EPS) * lng_ref[...] + lnb_ref[...]
    vb = v.astype(BF16)

    pos_i = lax.broadcasted_iota(jnp.int32, (SGU_BLOCK, SGU_BLOCK), 0) // CHUNK
    pos_j = lax.broadcasted_iota(jnp.int32, (SGU_BLOCK, SGU_BLOCK), 1) // CHUNK
    causal = pos_j <= pos_i
    rows = []
    for n in range(tm // SGU_BLOCK):
        cols = []
        for g in range(GMLP_GROUPS):
            w = jnp.where(causal, sw_ref[g], 0.0).astype(BF16)
            blk = vb[n * SGU_BLOCK:(n + 1) * SGU_BLOCK, g * gw:(g + 1) * gw]
            cols.append(jnp.dot(w, blk, preferred_element_type=F32) + sb_ref[g])
        rows.append(jnp.concatenate(cols, axis=1))
    sv = jnp.concatenate(rows, axis=0)
    gout_ref[...] = _rms(u * sv, og_ref[...]).astype(BF16)

    dd = DIFF_HEADS * LANES
    p = jnp.dot(hb, win_ref[:, 2 * dg:], preferred_element_type=F32)
    scale = DIFF_QK_DIM ** -0.5
    for h in range(DIFF_HEADS):
        q = p[:, h * LANES:(h + 1) * LANES] * scale
        k = p[:, dd + h * LANES:dd + (h + 1) * LANES]
        vv = p[:, 2 * dd + h * LANES:2 * dd + (h + 1) * LANES]
        k_ref[h] = k.astype(BF16)
        qt = q.T.astype(BF16)
        vt = vv.T.astype(BF16)
        for t in range(tm // ATT_TILE):
            qt_ref[h, t] = qt[:, t * ATT_TILE:(t + 1) * ATT_TILE]
            vt_ref[h, t] = vt[:, t * ATT_TILE:(t + 1) * ATT_TILE]


def _mix_in(x2, ng, win, lng, lnb, sw, sb, og):
    n, d = x2.shape
    dg = lng.shape[1]
    tm = min(TOKEN_TILE, n)
    nt = tm // ATT_TILE
    const = lambda *shape: pl.BlockSpec(shape, lambda i: (0,) * len(shape))
    return pl.pallas_call(
        _mix_in_kernel,
        grid=(n // tm,),
        in_specs=[
            pl.BlockSpec((tm, d), lambda i: (i, 0)),
            const(1, d),
            const(*win.shape),
            const(1, dg),
            const(1, dg),
            const(*sw.shape),
            const(*sb.shape),
            const(1, dg),
        ],
        out_specs=[
            pl.BlockSpec((tm, dg), lambda i: (i, 0)),
            pl.BlockSpec((DIFF_HEADS, nt, LANES, ATT_TILE), lambda i: (0, i, 0, 0)),
            pl.BlockSpec((DIFF_HEADS, tm, LANES), lambda i: (0, i, 0)),
            pl.BlockSpec((DIFF_HEADS, nt, LANES, ATT_TILE), lambda i: (0, i, 0, 0)),
        ],
        out_shape=[
            jax.ShapeDtypeStruct((n, dg), BF16),
            jax.ShapeDtypeStruct((DIFF_HEADS, n // ATT_TILE, LANES, ATT_TILE), BF16),
            jax.ShapeDtypeStruct((DIFF_HEADS, n, LANES), BF16),
            jax.ShapeDtypeStruct((DIFF_HEADS, n // ATT_TILE, LANES, ATT_TILE), BF16),
        ],
        compiler_params=pltpu.CompilerParams(
            dimension_semantics=("arbitrary",), vmem_limit_bytes=VMEM_LIMIT),
        name="mix_in",
    )(x2, ng, win, lng, lnb, sw, sb, og)


def _attn_kernel(lq1_ref, lk1_ref, lq2_ref, lk2_ref, qt_ref, k_ref, vt_ref, bias_ref, g_ref,
                 o_ref, acc_ref, m_ref, l_ref, *, lam_init):
    qi = pl.program_id(2)
    t = ATT_TILE
    qt = qt_ref[...]
    drow = lax.broadcasted_iota(jnp.int32, qt.shape, 0)
    zero = jnp.zeros_like(qt)
    q_maps = (jnp.where(drow < DIFF_QK_DIM, qt, zero), jnp.where(drow >= DIFF_QK_DIM, qt, zero))

    acc_ref[...] = jnp.zeros_like(acc_ref)
    m_ref[...] = jnp.full_like(m_ref, NEG_INF)
    l_ref[...] = jnp.zeros_like(l_ref)

    def tile(j, bias, masked):
        k = k_ref[pl.ds(pl.multiple_of(j * t, t), t), :]
        vt = vt_ref[j]
        if masked:
            kc = lax.broadcasted_iota(jnp.int32, (t, t), 0) // CHUNK
            qc = lax.broadcasted_iota(jnp.int32, (t, t), 1) // CHUNK
            visible = kc <= qc
        for m in range(2):
            s = jnp.dot(k, q_maps[m], preferred_element_type=F32)
            if bias is not None:
                s = s + bias
            if masked:
                s = jnp.where(visible, s, NEG_INF)
            m_old = m_ref[m]
            m_new = jnp.maximum(m_old, jnp.max(s, axis=0, keepdims=True))
            alpha = jnp.exp(m_old - m_new)
            p = jnp.exp(s - m_new)
            l_ref[m] = alpha * l_ref[m] + jnp.sum(p, axis=0, keepdims=True)
            m_ref[m] = m_new
            acc_ref[m] = alpha * acc_ref[m] + jnp.dot(vt, p.astype(BF16), preferred_element_type=F32)

    def far_body(j, carry):
        tile(j, None, False)
        return carry

    lax.fori_loop(0, jnp.maximum(qi - 1, 0), far_body, 0)

    @pl.when(qi >= 1)
    def _():
        tile(qi - 1, bias_ref[1], False)

    tile(qi, bias_ref[0], True)

    lam = (jnp.exp(jnp.sum(lq1_ref[...] * lk1_ref[...], keepdims=True))
           - jnp.exp(jnp.sum(lq2_ref[...] * lk2_ref[...], keepdims=True)) + lam_init)
    o = acc_ref[0] * (1.0 / l_ref[0]) - lam * (acc_ref[1] * (1.0 / l_ref[1]))
    o = o * lax.rsqrt(jnp.mean(o * o, axis=0, keepdims=True) + NORM_EPS)
    o_ref[...] = (o.T * (g_ref[...] * (1.0 - lam_init))).astype(BF16)


def _diff_attn(qt, k, vt, bias, lams, subln_g, batch, seq, lam_init):
    t = ATT_TILE
    nq = seq // t
    lam_spec = pl.BlockSpec((1, DIFF_QK_DIM), lambda b, h, i: (0, 0))
    return pl.pallas_call(
        functools.partial(_attn_kernel, lam_init=lam_init),
        grid=(batch, DIFF_HEADS, nq),
        in_specs=[
            lam_spec, lam_spec, lam_spec, lam_spec,
            pl.BlockSpec((None, None, LANES, t), lambda b, h, i: (h, b * nq + i, 0, 0)),
            pl.BlockSpec((None, seq, LANES), lambda b, h, i: (h, b, 0)),
            pl.BlockSpec((None, nq, LANES, t), lambda b, h, i: (h, b, 0, 0)),
            pl.BlockSpec((None, 2, t, t), lambda b, h, i: (h, 0, 0, 0)),
            pl.BlockSpec((1, DIFF_V_DIM), lambda b, h, i: (0, 0)),
        ],
        out_specs=pl.BlockSpec((t, DIFF_V_DIM), lambda b, h, i: (b * nq + i, h)),
        out_shape=jax.ShapeDtypeStruct((batch * seq, DIFF_HEADS * DIFF_V_DIM), BF16),
        scratch_shapes=[
            pltpu.VMEM((2, DIFF_V_DIM, t), F32),
            pltpu.VMEM((2, 1, t), F32),
            pltpu.VMEM((2, 1, t), F32),
        ],
        compiler_params=pltpu.CompilerParams(
            dimension_semantics=("arbitrary", "arbitrary", "arbitrary"), vmem_limit_bytes=VMEM_LIMIT),
        name="diff_attn",
    )(*lams, qt, k, vt, bias, subln_g)


def _rel_bucket(rel):
    nb = REL_BUCKETS // 2
    max_exact = nb // 2
    side = jnp.where(rel > 0, nb, 0)
    n = jnp.abs(rel)
    nf = jnp.maximum(n, 1).astype(F32)
    large = max_exact + (jnp.log(nf / max_exact) / math.log(REL_MAX_DIST / max_exact)
                         * (nb - max_exact)).astype(jnp.int32)
    large = jnp.minimum(large, nb - 1)
    return side + jnp.where(n < max_exact, n, large)


def _near_bias_tiles(rel_table):
    t = ATT_TILE
    kpos = jnp.arange(t)[:, None]
    qpos = jnp.arange(t)[None, :]
    rel = jnp.stack([kpos - qpos, kpos - qpos - t])
    bias = rel_table[_rel_bucket(rel)].astype(F32)
    far = rel_table[REL_BUCKETS // 2 - 1].astype(F32)
    return (bias - far).transpose(3, 0, 1, 2)


def _mix_out_kernel(x_ref, g_ref, d_ref, wo_ref, ng_ref, rwt_ref, rb_ref,
                    x1_ref, h2_ref, e_ref, gate_ref, rank_ref, cnt_ref, base_ref):
    i = pl.program_id(0)
    tm = x_ref.shape[0]
    dg = g_ref.shape[1]

    @pl.when(i == 0)
    def _():
        base_ref[...] = jnp.zeros_like(base_ref)

    mix = (jnp.dot(g_ref[...], wo_ref[:dg, :], preferred_element_type=F32)
           + jnp.dot(d_ref[...], wo_ref[dg:, :], preferred_element_type=F32))
    x1 = x_ref[...] + mix
    x1_ref[...] = x1
    h2 = _rms(x1, ng_ref[...])
    h2_ref[...] = h2

    logits = lax.dot_general(rwt_ref[...], h2, (((1,), (1,)), ((), ())),
                             precision=lax.Precision.HIGHEST,
                             preferred_element_type=F32) + rb_ref[...]
    eidx = lax.broadcasted_iota(jnp.int32, logits.shape, 0)
    cur = logits
    vals, sels = [], []
    for k in range(TOP_K):
        mx = jnp.max(cur, axis=0, keepdims=True)
        idx = jnp.min(jnp.where(cur == mx, eidx, N_EXPERTS), axis=0, keepdims=True)
        sel = eidx == idx
        vals.append(mx)
        sels.append(sel)
        e_ref[k:k + 1, :] = idx
        cur = jnp.where(sel, -jnp.inf, cur)
    ex = [jnp.exp(v - vals[0]) for v in vals]
    inv = 1.0 / (ex[0] + ex[1] + ex[2] + ex[3])
    for k in range(TOP_K):
        gate_ref[k:k + 1, :] = ex[k] * inv

    member = sels[0] | sels[1] | sels[2] | sels[3]
    mem_b = jnp.where(member, 1.0, 0.0).astype(BF16)
    ri = lax.broadcasted_iota(jnp.int32, (tm, tm), 0)
    ci = lax.broadcasted_iota(jnp.int32, (tm, tm), 1)
    tri = jnp.where(ri < ci, 1.0, 0.0).astype(BF16)
    excl = jnp.dot(mem_b, tri, preferred_element_type=F32)
    rank = base_ref[...] + excl
    for k in range(TOP_K):
        rank_ref[k:k + 1, :] = jnp.sum(jnp.where(sels[k], rank, 0.0), axis=0, keepdims=True).astype(jnp.int32)
    base_new = base_ref[...] + jnp.sum(mem_b.astype(F32), axis=1, keepdims=True)
    base_ref[...] = base_new
    cnt_ref[...] = jnp.broadcast_to(base_new, cnt_ref.shape).astype(jnp.int32)


def _mix_out(x2, gout, dout, wo, ng, rwt, rb):
    n, d = x2.shape
    dg = gout.shape[1]
    tm = min(TOKEN_TILE, n)
    const = lambda *shape: pl.BlockSpec(shape, lambda i: (0,) * len(shape))
    tok = lambda w: pl.BlockSpec((tm, w), lambda i: (i, 0))
    slot = pl.BlockSpec((TOP_K, tm), lambda i: (0, i))
    return pl.pallas_call(
        _mix_out_kernel,
        grid=(n // tm,),
        in_specs=[tok(d), tok(dg), tok(dout.shape[1]), const(*wo.shape), const(1, d),
                  const(*rwt.shape), const(N_EXPERTS, 1)],
        out_specs=[tok(d), tok(d), slot, slot, slot, const(N_EXPERTS, LANES)],
        out_shape=[
            jax.ShapeDtypeStruct((n, d), F32),
            jax.ShapeDtypeStruct((n, d), F32),
            jax.ShapeDtypeStruct((TOP_K, n), jnp.int32),
            jax.ShapeDtypeStruct((TOP_K, n), F32),
            jax.ShapeDtypeStruct((TOP_K, n), jnp.int32),
            jax.ShapeDtypeStruct((N_EXPERTS, LANES), jnp.int32),
        ],
        scratch_shapes=[pltpu.VMEM((N_EXPERTS, 1), F32)],
        compiler_params=pltpu.CompilerParams(
            dimension_semantics=("arbitrary",), vmem_limit_bytes=VMEM_LIMIT),
        name="mix_out",
    )(x2, gout, dout, wo, ng, rwt, rb)


def _row_copy(src_ref, src_row, dst_ref, dst_row, sem):
    return pltpu.make_async_copy(src_ref.at[pl.ds(src_row, 1), :], dst_ref.at[pl.ds(dst_row, 1), :], sem)


def _dispatch_kernel(pos_ref, h_ref, xs_in_ref, xs_ref, sem):
    del xs_in_ref
    tr = pos_ref.shape[1]
    base = pl.program_id(0) * tr

    def issue(t, carry):
        for k in range(TOP_K):
            _row_copy(h_ref, base + t, xs_ref, pos_ref[k, t], sem).start()
        return carry

    lax.fori_loop(0, tr, issue, 0, unroll=8)
    pltpu.make_async_copy(h_ref.at[pl.ds(0, TOP_K * tr), :], xs_ref.at[pl.ds(0, TOP_K * tr), :], sem).wait()


def _dispatch(pos3, h2, xs_init):
    n, d = h2.shape
    tr = pos3.shape[2]
    return pl.pallas_call(
        _dispatch_kernel,
        grid=(n // tr,),
        in_specs=[
            pl.BlockSpec((None, TOP_K, tr), lambda i: (i, 0, 0), memory_space=pltpu.SMEM),
            pl.BlockSpec(memory_space=pl.ANY),
            pl.BlockSpec(memory_space=pl.ANY),
        ],
        out_specs=pl.BlockSpec(memory_space=pl.ANY),
        out_shape=jax.ShapeDtypeStruct(xs_init.shape, xs_init.dtype),
        scratch_shapes=[pltpu.SemaphoreType.DMA],
        input_output_aliases={2: 0},
        compiler_params=pltpu.CompilerParams(dimension_semantics=("arbitrary",)),
        name="dispatch",
    )(pos3, h2, xs_init)


def _expert_kernel(bexp_ref, nact_ref, xs_ref, w1g_ref, w1l_ref, b1g_ref, b1l_ref, w2_ref, b2_ref, y_ref):
    del bexp_ref
    active = pl.program_id(0) < nact_ref[0]

    @pl.when(jnp.logical_not(active))
    def _():
        y_ref[...] = jnp.zeros_like(y_ref)

    @pl.when(active)
    def _():
        x = xs_ref[...].astype(BF16)
        ag = jnp.dot(x, w1g_ref[...], preferred_element_type=F32) + b1g_ref[...]
        al = jnp.dot(x, w1l_ref[...], preferred_element_type=F32) + b1l_ref[...]
        ag = jnp.minimum(ag, SWIGLU_LIMIT)
        al = jnp.clip(al, -SWIGLU_LIMIT, SWIGLU_LIMIT)
        act = ag * (1.0 / (1.0 + jnp.exp(-SWIGLU_ALPHA * ag))) * (al + 1.0)
        y_ref[...] = jnp.dot(act.astype(BF16), w2_ref[...], preferred_element_type=F32) + b2_ref[...]


def _experts(bexp, nact, xs, w1g, w1l, b1g, b1l, w2, b2):
    cap, d = xs.shape
    de = w1g.shape[2]
    nblk = cap // MOE_BLOCK
    row = lambda b, be, na: (jnp.minimum(b, na[0] - 1), 0)
    exp3 = lambda b, be, na: (be[b], 0, 0)
    return pl.pallas_call(
        _expert_kernel,
        grid_spec=pltpu.PrefetchScalarGridSpec(
            num_scalar_prefetch=2,
            grid=(nblk,),
            in_specs=[
                pl.BlockSpec((MOE_BLOCK, d), row),
                pl.BlockSpec((None, d, de), exp3),
                pl.BlockSpec((None, d, de), exp3),
                pl.BlockSpec((None, 1, de), exp3),
                pl.BlockSpec((None, 1, de), exp3),
                pl.BlockSpec((None, de, d), exp3),
                pl.BlockSpec((None, 1, d), exp3),
            ],
            out_specs=pl.BlockSpec((MOE_BLOCK, d), lambda b, be, na: (b, 0)),
        ),
        out_shape=jax.ShapeDtypeStruct((cap, d), F32),
        compiler_params=pltpu.CompilerParams(
            dimension_semantics=("arbitrary",), vmem_limit_bytes=VMEM_LIMIT),
        name="experts",
    )(bexp, nact, xs, w1g, w1l, b1g, b1l, w2, b2)


def _combine_kernel(pos_ref, y_ref, x1_ref, gate_ref, fg_ref, o_ref, ybuf_ref, sem):
    tr = x1_ref.shape[0]

    def issue(t, carry):
        for k in range(TOP_K):
            _row_copy(y_ref, pos_ref[k, t], ybuf_ref, k * tr + t, sem).start()
        return carry

    lax.fori_loop(0, tr, issue, 0, unroll=8)
    pltpu.make_async_copy(y_ref.at[pl.ds(0, TOP_K * tr), :], ybuf_ref, sem).wait()

    x = x1_ref[...]
    gates = gate_ref[...]
    for k in range(TOP_K):
        x = x + gates[:, k:k + 1] * ybuf_ref[k * tr:(k + 1) * tr, :]
    o_ref[...] = _rms(x, fg_ref[...])


def _combine(pos3, y, x1, gate_col, fg):
    n, d = x1.shape
    tr = pos3.shape[2]
    return pl.pallas_call(
        _combine_kernel,
        grid=(n // tr,),
        in_specs=[
            pl.BlockSpec((None, TOP_K, tr), lambda i: (i, 0, 0), memory_space=pltpu.SMEM),
            pl.BlockSpec(memory_space=pl.ANY),
            pl.BlockSpec((tr, d), lambda i: (i, 0)),
            pl.BlockSpec((tr, TOP_K), lambda i: (i, 0)),
            pl.BlockSpec((1, d), lambda i: (0, 0)),
        ],
        out_specs=pl.BlockSpec((tr, d), lambda i: (i, 0)),
        out_shape=jax.ShapeDtypeStruct((n, d), F32),
        scratch_shapes=[pltpu.VMEM((TOP_K * tr, d), F32), pltpu.SemaphoreType.DMA],
        compiler_params=pltpu.CompilerParams(
            dimension_semantics=("arbitrary",), vmem_limit_bytes=VMEM_LIMIT),
        name="combine",
    )(pos3, y, x1, gate_col, fg)


def _layer(x2, batch, seq, layer, lam_init, rel_bias_table, mix_norm_g, w_in, sgu_ln_g, sgu_ln_b, sgu_w,
           sgu_b, gmlp_out_g, lambda_q1, lambda_k1, lambda_q2, lambda_k2, diff_subln_g, w_out,
           ffn_norm_g, router_w, router_b, w_mlp1, b_mlp1, w_mlp2, b_mlp2):
    n, d = x2.shape
    l = layer
    row = lambda a: a[l].reshape(1, -1).astype(F32)

    gout, qt, k, vt = _mix_in(x2, row(mix_norm_g), w_in[l].astype(BF16), row(sgu_ln_g), row(sgu_ln_b),
                              sgu_w[l].astype(F32), sgu_b[l].astype(F32)[:, :, None], row(gmlp_out_g))
    dout = _diff_attn(qt, k, vt, _near_bias_tiles(rel_bias_table),
                      (row(lambda_q1), row(lambda_k1), row(lambda_q2), row(lambda_k2)),
                      row(diff_subln_g), batch, seq, lam_init)
    x1, h2, top_e, gate, rank, counts = _mix_out(
        x2, gout, dout, w_out[l].astype(BF16), row(ffn_norm_g),
        router_w[l].astype(F32).T, router_b[l].astype(F32)[:, None])

    counts = counts[:, 0]
    padded = (counts + MOE_BLOCK - 1) // MOE_BLOCK * MOE_BLOCK
    pad_end = jnp.cumsum(padded)
    pad_start = pad_end - padded
    nblk = n * TOP_K // MOE_BLOCK + N_EXPERTS
    cap = nblk * MOE_BLOCK
    block_start = jnp.arange(nblk, dtype=jnp.int32) * MOE_BLOCK
    block_exp = jnp.minimum(jnp.sum(pad_end[None, :] <= block_start[:, None], axis=1),
                            N_EXPERTS - 1).astype(jnp.int32)
    n_active = (pad_end[-1] // MOE_BLOCK).astype(jnp.int32).reshape(1)
    pos = pad_start[top_e] + rank
    tr = min(ROW_TILE, n)
    pos3 = pos.reshape(TOP_K, n // tr, tr).transpose(1, 0, 2)

    xs = _dispatch(pos3, h2, jnp.zeros((cap, d), F32))
    w1 = w_mlp1[l]
    b1 = b_mlp1[l]
    y = _experts(block_exp, n_active, xs,
                 w1[:, :, 0::2].astype(BF16), w1[:, :, 1::2].astype(BF16),
                 b1[:, None, 0::2].astype(F32), b1[:, None, 1::2].astype(F32),
                 w_mlp2[l].astype(BF16), b_mlp2[l][:, None, :].astype(F32))
    return pos3, y, x1, gate.T


def kernel(x, rel_bias_table, mix_norm_g, w_in, sgu_ln_g, sgu_ln_b, sgu_w, sgu_b, gmlp_out_g, lambda_q1,
           lambda_k1, lambda_q2, lambda_k2, diff_subln_g, w_out, ffn_norm_g, router_w, router_b, w_mlp1,
           b_mlp1, w_mlp2, b_mlp2, final_norm_g):
    batch, seq, d = x.shape
    depth = w_in.shape[0]
    assert depth == 1, "the final RMSNorm is fused into the last layer's combine step"
    assert seq % TOKEN_TILE == 0 or batch * seq <= TOKEN_TILE
    x2 = x.reshape(batch * seq, d).astype(F32)
    lam_init = 0.8 - 0.6 * math.exp(-0.3 * 0)
    pos3, y, x1, gate_col = _layer(
        x2, batch, seq, 0, lam_init, rel_bias_table, mix_norm_g, w_in, sgu_ln_g, sgu_ln_b, sgu_w, sgu_b,
        gmlp_out_g, lambda_q1, lambda_k1, lambda_q2, lambda_k2, diff_subln_g, w_out, ffn_norm_g,
        router_w, router_b, w_mlp1, b_mlp1, w_mlp2, b_mlp2)
    out = _combine(pos3, y, x1, gate_col, final_norm_g.reshape(1, d).astype(F32))
    return out.reshape(batch, seq, d).astype(x.dtype)
```

```python
import functools
import math

import jax
import jax.numpy as jnp
from jax import lax
from jax.experimental import pallas as pl
from jax.experimental.pallas import tpu as pltpu

F32 = jnp.float32
BF16 = jnp.bfloat16

CHUNK = 64
GMLP_GROUPS = 4
SGU_BLOCK = 128
DIFF_HEADS = 4
DIFF_QK_DIM = 64
DIFF_V_DIM = 128
REL_BUCKETS = 32
REL_MAX_DIST = 128
N_EXPERTS = 32
TOP_K = 4
SWIGLU_ALPHA = 1.702
SWIGLU_LIMIT = 7.0
NORM_EPS = 1e-5
NEG_INF = -1e30

LANES = 128
MXU_WIDTH = 256
ATT_TILE = 512
LOG2E = 1.4426950408889634
TOKEN_TILE = 512
MOE_BLOCK = 256
ROW_TILE = 256
VMEM_LIMIT = 56 * 1024 * 1024


def _rms(x, g):
    return x * lax.rsqrt(jnp.mean(x * x, axis=-1, keepdims=True) + NORM_EPS) * g


def _gelu(z):
    return 0.5 * z * (1.0 + lax.erf(z * (1.0 / math.sqrt(2.0))))


def _mix_in_kernel(x_ref, ng_ref, win_ref, lng_ref, lnb_ref, sw_ref, sb_ref, og_ref,
                   gout_ref, qt_ref, k_ref, vt_ref):
    tm = x_ref.shape[0]
    dg = gout_ref.shape[1]
    gw = dg // GMLP_GROUPS
    hb = _rms(x_ref[...], ng_ref[...]).astype(BF16)

    z = _gelu(jnp.dot(hb, win_ref[:, :2 * dg], preferred_element_type=F32))
    u = z[:, :dg]
    v = z[:, dg:]
    mu = jnp.mean(v, axis=-1, keepdims=True)
    vc = v - mu
    v = vc * lax.rsqrt(jnp.mean(vc * vc, axis=-1, keepdims=True) + NORM_EPS) * lng_ref[...] + lnb_ref[...]
    vb = v.astype(BF16)

    pos_i = lax.broadcasted_iota(jnp.int32, (SGU_BLOCK, SGU_BLOCK), 0) // CHUNK
    pos_j = lax.broadcasted_iota(jnp.int32, (SGU_BLOCK, SGU_BLOCK), 1) // CHUNK
    causal = pos_j <= pos_i
    rows = []
    for n in range(tm // SGU_BLOCK):
        cols = []
        for g in range(GMLP_GROUPS):
            w = jnp.where(causal, sw_ref[g], 0.0).astype(BF16)
            blk = vb[n * SGU_BLOCK:(n + 1) * SGU_BLOCK, g * gw:(g + 1) * gw]
            cols.append(jnp.dot(w, blk, preferred_element_type=F32) + sb_ref[g])
        rows.append(jnp.concatenate(cols, axis=1))
    sv = jnp.concatenate(rows, axis=0)
    gout_ref[...] = _rms(u * sv, og_ref[...]).astype(BF16)

    dd = DIFF_HEADS * LANES
    p = jnp.dot(hb, win_ref[:, 2 * dg:], preferred_element_type=F32)
    scale = DIFF_QK_DIM ** -0.5 * LOG2E
    for h in range(DIFF_HEADS):
        q = p[:, h * LANES:(h + 1) * LANES] * scale
        k = p[:, dd + h * LANES:dd + (h + 1) * LANES]
        vv = p[:, 2 * dd + h * LANES:2 * dd + (h + 1) * LANES]
        k_ref[h] = k.astype(BF16)
        qt = q.T.astype(BF16)
        vt = vv.T.astype(BF16)
        for t in range(tm // ATT_TILE):
            qt_ref[h, t] = qt[:, t * ATT_TILE:(t + 1) * ATT_TILE]
            vt_ref[h, t] = vt[:, t * ATT_TILE:(t + 1) * ATT_TILE]


def _mix_in(x2, ng, win, lng, lnb, sw, sb, og):
    n, d = x2.shape
    dg = lng.shape[1]
    tm = min(TOKEN_TILE, n)
    nt = tm // ATT_TILE
    const = lambda *shape: pl.BlockSpec(shape, lambda i: (0,) * len(shape))
    return pl.pallas_call(
        _mix_in_kernel,
        grid=(n // tm,),
        in_specs=[
            pl.BlockSpec((tm, d), lambda i: (i, 0)),
            const(1, d),
            const(*win.shape),
            const(1, dg),
            const(1, dg),
            const(*sw.shape),
            const(*sb.shape),
            const(1, dg),
        ],
        out_specs=[
            pl.BlockSpec((tm, dg), lambda i: (i, 0)),
            pl.BlockSpec((DIFF_HEADS, nt, LANES, ATT_TILE), lambda i: (0, i, 0, 0)),
            pl.BlockSpec((DIFF_HEADS, tm, LANES), lambda i: (0, i, 0)),
            pl.BlockSpec((DIFF_HEADS, nt, LANES, ATT_TILE), lambda i: (0, i, 0, 0)),
        ],
        out_shape=[
            jax.ShapeDtypeStruct((n, dg), BF16),
            jax.ShapeDtypeStruct((DIFF_HEADS, n // ATT_TILE, LANES, ATT_TILE), BF16),
            jax.ShapeDtypeStruct((DIFF_HEADS, n, LANES), BF16),
            jax.ShapeDtypeStruct((DIFF_HEADS, n // ATT_TILE, LANES, ATT_TILE), BF16),
        ],
        compiler_params=pltpu.CompilerParams(
            dimension_semantics=("arbitrary",), vmem_limit_bytes=VMEM_LIMIT),
        name="mix_in",
    )(x2, ng, win, lng, lnb, sw, sb, og)


def _attn_kernel(lq1_ref, lk1_ref, lq2_ref, lk2_ref, qt_ref, k_ref, vt_ref, bias_ref, g_ref,
                 o_ref, acc_ref, m_ref, l_ref, *, lam_init):
    qi = pl.program_id(2)
    t = ATT_TILE
    qt = qt_ref[...]
    drow = lax.broadcasted_iota(jnp.int32, qt.shape, 0)
    zero = jnp.zeros_like(qt)
    qw = jnp.concatenate([jnp.where(drow < DIFF_QK_DIM, qt, zero),
                          jnp.where(drow >= DIFF_QK_DIM, qt, zero)], axis=1)

    acc_ref[...] = jnp.zeros_like(acc_ref)
    m_ref[...] = jnp.full_like(m_ref, NEG_INF)
    l_ref[...] = jnp.zeros_like(l_ref)

    def tile(j, bias, masked):
        k = k_ref[pl.ds(pl.multiple_of(j * t, t), t), :]
        s = jnp.dot(k, qw, preferred_element_type=F32)
        if bias is not None:
            halves = [s[:, :t] + bias, s[:, t:] + bias]
            if masked:
                kc = lax.broadcasted_iota(jnp.int32, (t, t), 0) // CHUNK
                qc = lax.broadcasted_iota(jnp.int32, (t, t), 1) // CHUNK
                halves = [jnp.where(kc <= qc, h, NEG_INF) for h in halves]
            s = jnp.concatenate(halves, axis=1)
        m_old = m_ref[...]
        m_new = jnp.maximum(m_old, jnp.max(s, axis=0, keepdims=True))
        alpha = jnp.exp2(m_old - m_new)
        p = jnp.exp2(s - m_new)
        l_ref[...] = alpha * l_ref[...] + jnp.sum(p, axis=0, keepdims=True)
        m_ref[...] = m_new
        acc_ref[...] = alpha * acc_ref[...] + jnp.dot(vt_ref[j], p.astype(BF16), preferred_element_type=F32)

    def far_body(j, carry):
        tile(j, None, False)
        return carry

    lax.fori_loop(0, jnp.maximum(qi - 1, 0), far_body, 0)

    @pl.when(qi >= 1)
    def _():
        tile(qi - 1, bias_ref[1], False)

    tile(qi, bias_ref[0], True)

    lam = (jnp.exp(jnp.sum(lq1_ref[...] * lk1_ref[...], keepdims=True))
           - jnp.exp(jnp.sum(lq2_ref[...] * lk2_ref[...], keepdims=True)) + lam_init)
    on = acc_ref[...] * (1.0 / l_ref[...])
    o = on[:, :t] - lam * on[:, t:]
    o = o * lax.rsqrt(jnp.mean(o * o, axis=0, keepdims=True) + NORM_EPS)
    o_ref[...] = (o.T * (g_ref[...] * (1.0 - lam_init))).astype(BF16)


def _diff_attn(qt, k, vt, bias, lams, subln_g, batch, seq, lam_init):
    t = ATT_TILE
    nq = seq // t
    lam_spec = pl.BlockSpec((1, DIFF_QK_DIM), lambda b, h, i: (0, 0))
    return pl.pallas_call(
        functools.partial(_attn_kernel, lam_init=lam_init),
        grid=(batch, DIFF_HEADS, nq),
        in_specs=[
            lam_spec, lam_spec, lam_spec, lam_spec,
            pl.BlockSpec((None, None, LANES, t), lambda b, h, i: (h, b * nq + i, 0, 0)),
            pl.BlockSpec((None, seq, LANES), lambda b, h, i: (h, b, 0)),
            pl.BlockSpec((None, nq, LANES, t), lambda b, h, i: (h, b, 0, 0)),
            pl.BlockSpec((None, 2, t, t), lambda b, h, i: (h, 0, 0, 0)),
            pl.BlockSpec((1, DIFF_V_DIM), lambda b, h, i: (0, 0)),
        ],
        out_specs=pl.BlockSpec((t, DIFF_V_DIM), lambda b, h, i: (b * nq + i, h)),
        out_shape=jax.ShapeDtypeStruct((batch * seq, DIFF_HEADS * DIFF_V_DIM), BF16),
        scratch_shapes=[
            pltpu.VMEM((DIFF_V_DIM, 2 * t), F32),
            pltpu.VMEM((1, 2 * t), F32),
            pltpu.VMEM((1, 2 * t), F32),
        ],
        compiler_params=pltpu.CompilerParams(
            dimension_semantics=("arbitrary", "arbitrary", "arbitrary"), vmem_limit_bytes=VMEM_LIMIT),
        name="diff_attn",
    )(*lams, qt, k, vt, bias, subln_g)


def _rel_bucket(rel):
    nb = REL_BUCKETS // 2
    max_exact = nb // 2
    side = jnp.where(rel > 0, nb, 0)
    n = jnp.abs(rel)
    nf = jnp.maximum(n, 1).astype(F32)
    large = max_exact + (jnp.log(nf / max_exact) / math.log(REL_MAX_DIST / max_exact)
                         * (nb - max_exact)).astype(jnp.int32)
    large = jnp.minimum(large, nb - 1)
    return side + jnp.where(n < max_exact, n, large)


def _near_bias_tiles(rel_table):
    t = ATT_TILE
    kpos = jnp.arange(t)[:, None]
    qpos = jnp.arange(t)[None, :]
    rel = jnp.stack([kpos - qpos, kpos - qpos - t])
    onehot = _rel_bucket(rel)[..., None] == jnp.arange(REL_BUCKETS)
    table = rel_table.astype(F32)
    far = table[REL_BUCKETS // 2 - 1]
    tiles = [jnp.sum(jnp.where(onehot, table[:, h], 0.0), axis=-1) - far[h] for h in range(DIFF_HEADS)]
    return jnp.stack(tiles) * LOG2E


def _mix_out_kernel(x_ref, g_ref, d_ref, wo_ref, ng_ref, rwt_ref, rb_ref,
                    x1_ref, h2_ref, e_ref, gate_ref, rank_ref, cnt_ref, base_ref):
    i = pl.program_id(0)
    tm = x_ref.shape[0]
    dg = g_ref.shape[1]

    @pl.when(i == 0)
    def _():
        base_ref[...] = jnp.zeros_like(base_ref)

    mix = (jnp.dot(g_ref[...], wo_ref[:dg, :], preferred_element_type=F32)
           + jnp.dot(d_ref[...], wo_ref[dg:, :], preferred_element_type=F32))
    x1 = x_ref[...] + mix
    x1_ref[...] = x1
    h2 = _rms(x1, ng_ref[...])
    h2_ref[...] = h2

    logits = lax.dot_general(rwt_ref[...], h2, (((1,), (1,)), ((), ())),
                             precision=lax.Precision.HIGHEST,
                             preferred_element_type=F32) + rb_ref[...]
    eidx = lax.broadcasted_iota(jnp.int32, logits.shape, 0)
    cur = logits
    vals, sels = [], []
    for k in range(TOP_K):
        mx = jnp.max(cur, axis=0, keepdims=True)
        idx = jnp.min(jnp.where(cur == mx, eidx, N_EXPERTS), axis=0, keepdims=True)
        sel = eidx == idx
        vals.append(mx)
        sels.append(sel)
        e_ref[k:k + 1, :] = idx
        cur = jnp.where(sel, -jnp.inf, cur)
    ex = [jnp.exp(v - vals[0]) for v in vals]
    inv = 1.0 / (ex[0] + ex[1] + ex[2] + ex[3])
    for k in range(TOP_K):
        gate_ref[k:k + 1, :] = ex[k] * inv

    member = sels[0] | sels[1] | sels[2] | sels[3]
    mem_b = jnp.where(member, 1.0, 0.0).astype(BF16)
    ri = lax.broadcasted_iota(jnp.int32, (tm, tm), 0)
    ci = lax.broadcasted_iota(jnp.int32, (tm, tm), 1)
    tri = jnp.where(ri < ci, 1.0, 0.0).astype(BF16)
    excl = jnp.dot(mem_b, tri, preferred_element_type=F32)
    rank = base_ref[...] + excl
    for k in range(TOP_K):
        rank_ref[k:k + 1, :] = jnp.sum(jnp.where(sels[k], rank, 0.0), axis=0, keepdims=True).astype(jnp.int32)
    base_new = base_ref[...] + jnp.sum(mem_b.astype(F32), axis=1, keepdims=True)
    base_ref[...] = base_new
    cnt_ref[...] = jnp.broadcast_to(base_new, cnt_ref.shape).astype(jnp.int32)


def _mix_out(x2, gout, dout, wo, ng, rwt, rb):
    n, d = x2.shape
    dg = gout.shape[1]
    tm = min(TOKEN_TILE, n)
    const = lambda *shape: pl.BlockSpec(shape, lambda i: (0,) * len(shape))
    tok = lambda w: pl.BlockSpec((tm, w), lambda i: (i, 0))
    slot = pl.BlockSpec((TOP_K, tm), lambda i: (0, i))
    return pl.pallas_call(
        _mix_out_kernel,
        grid=(n // tm,),
        in_specs=[tok(d), tok(dg), tok(dout.shape[1]), const(*wo.shape), const(1, d),
                  const(*rwt.shape), const(N_EXPERTS, 1)],
        out_specs=[tok(d), tok(d), slot, slot, slot, const(N_EXPERTS, LANES)],
        out_shape=[
            jax.ShapeDtypeStruct((n, d), F32),
            jax.ShapeDtypeStruct((n, d), F32),
            jax.ShapeDtypeStruct((TOP_K, n), jnp.int32),
            jax.ShapeDtypeStruct((TOP_K, n), F32),
            jax.ShapeDtypeStruct((TOP_K, n), jnp.int32),
            jax.ShapeDtypeStruct((N_EXPERTS, LANES), jnp.int32),
        ],
        scratch_shapes=[pltpu.VMEM((N_EXPERTS, 1), F32)],
        compiler_params=pltpu.CompilerParams(
            dimension_semantics=("arbitrary",), vmem_limit_bytes=VMEM_LIMIT),
        name="mix_out",
    )(x2, gout, dout, wo, ng, rwt, rb)


def _row_copy(src_ref, src_row, dst_ref, dst_row, sem):
    return pltpu.make_async_copy(src_ref.at[pl.ds(src_row, 1), :], dst_ref.at[pl.ds(dst_row, 1), :], sem)


def _dispatch_kernel(cnt_ref, pstart_ref, padded_ref, pos_ref, h_ref, xs_ref, zrow_ref, sem, zsem):
    tr = h_ref.shape[0]

    @pl.when(pl.program_id(0) == 0)
    def _():
        zrow_ref[...] = jnp.zeros_like(zrow_ref)

        def pad_rows(start):
            def per_expert(e, carry):
                def per_row(r, c):
                    cp = _row_copy(zrow_ref, 0, xs_ref, pstart_ref[e] + r, zsem)
                    cp.start() if start else cp.wait()
                    return c
                lax.fori_loop(cnt_ref[e], padded_ref[e], per_row, 0)
                return carry
            lax.fori_loop(0, N_EXPERTS, per_expert, 0)

            zr = zrow_ref.shape[0]
            used = pstart_ref[N_EXPERTS - 1] + padded_ref[N_EXPERTS - 1]
            def per_tail(r, c):
                cp = pltpu.make_async_copy(
                    zrow_ref, xs_ref.at[pl.ds(pl.multiple_of(used + r * zr, zr), zr), :], zsem)
                cp.start() if start else cp.wait()
                return c
            lax.fori_loop(0, (xs_ref.shape[0] - used) // zr, per_tail, 0)

        pad_rows(True)
        pad_rows(False)

    def issue(t, carry):
        for k in range(TOP_K):
            _row_copy(h_ref, t, xs_ref, pos_ref[k, t], sem).start(priority=k % 2)
        return carry

    lax.fori_loop(0, tr, issue, 0, unroll=8)
    for k in range(TOP_K):
        pltpu.make_async_copy(h_ref, xs_ref.at[pl.ds(0, tr), :], sem).wait()


def _dispatch(counts, pad_start, padded, pos3, h2, cap):
    n, d = h2.shape
    tr = pos3.shape[2]
    return pl.pallas_call(
        _dispatch_kernel,
        grid_spec=pltpu.PrefetchScalarGridSpec(
            num_scalar_prefetch=3,
            grid=(n // tr,),
            in_specs=[
                pl.BlockSpec((None, TOP_K, tr), lambda i, *_: (i, 0, 0), memory_space=pltpu.SMEM),
                pl.BlockSpec((tr, d), lambda i, *_: (i, 0)),
            ],
            out_specs=pl.BlockSpec(memory_space=pl.ANY),
            scratch_shapes=[pltpu.VMEM((8, d), F32), pltpu.SemaphoreType.DMA, pltpu.SemaphoreType.DMA],
        ),
        out_shape=jax.ShapeDtypeStruct((cap, d), F32),
        compiler_params=pltpu.CompilerParams(dimension_semantics=("arbitrary",)),
        name="dispatch",
    )(counts, pad_start, padded, pos3, h2)


def _expert_kernel(bexp_ref, nact_ref, xs_ref, w1_ref, b1_ref, w2_ref, b2_ref, y_ref, w1p_ref, w2b_ref):
    b = pl.program_id(0)
    active = b < nact_ref[0]
    fresh = jnp.logical_or(b == 0, bexp_ref[b] != bexp_ref[jnp.maximum(b - 1, 0)])
    gw = MXU_WIDTH
    half = gw // 2

    @pl.when(jnp.logical_not(active))
    def _():
        y_ref[...] = jnp.zeros_like(y_ref)

    @pl.when(jnp.logical_and(active, fresh))
    def _():
        r = lax.broadcasted_iota(jnp.int32, (gw, gw), 0)
        c = lax.broadcasted_iota(jnp.int32, (gw, gw), 1)
        perm = jnp.where(r == jnp.where(c < half, 2 * c, 2 * (c - half) + 1), 1.0, 0.0).astype(BF16)
        for j in range(w1_ref.shape[1] // gw):
            w = w1_ref[:, j * gw:(j + 1) * gw].astype(BF16)
            w1p_ref[:, j * gw:(j + 1) * gw] = jnp.dot(w, perm, preferred_element_type=F32).astype(BF16)
        w2b_ref[...] = w2_ref[...].astype(BF16)

    @pl.when(active)
    def _():
        x = xs_ref[...].astype(BF16)
        a = jnp.dot(x, w1p_ref[...], preferred_element_type=F32) + b1_ref[...]
        acts = []
        for j in range(a.shape[1] // gw):
            ag = jnp.minimum(a[:, j * gw:j * gw + half], SWIGLU_LIMIT)
            al = jnp.clip(a[:, j * gw + half:(j + 1) * gw], -SWIGLU_LIMIT, SWIGLU_LIMIT)
            acts.append((ag * (1.0 / (1.0 + jnp.exp(-SWIGLU_ALPHA * ag))) * (al + 1.0)).astype(BF16))
        act = jnp.concatenate(acts, axis=1)
        y_ref[...] = jnp.dot(act, w2b_ref[...], preferred_element_type=F32) + b2_ref[...]


def _experts(bexp, nact, xs, w1, b1p, w2, b2):
    cap, d = xs.shape
    de2 = w1.shape[2]
    de = w2.shape[1]
    nblk = cap // MOE_BLOCK
    row = lambda b, be, na: (jnp.minimum(b, na[0] - 1), 0)
    exp3 = lambda b, be, na: (be[b], 0, 0)
    return pl.pallas_call(
        _expert_kernel,
        grid_spec=pltpu.PrefetchScalarGridSpec(
            num_scalar_prefetch=2,
            grid=(nblk,),
            in_specs=[
                pl.BlockSpec((MOE_BLOCK, d), row),
                pl.BlockSpec((None, d, de2), exp3),
                pl.BlockSpec((None, 1, de2), exp3),
                pl.BlockSpec((None, de, d), exp3),
                pl.BlockSpec((None, 1, d), exp3),
            ],
            out_specs=pl.BlockSpec((MOE_BLOCK, d), lambda b, be, na: (b, 0)),
            scratch_shapes=[pltpu.VMEM((d, de2), BF16), pltpu.VMEM((de, d), BF16)],
        ),
        out_shape=jax.ShapeDtypeStruct((cap, d), F32),
        compiler_params=pltpu.CompilerParams(
            dimension_semantics=("arbitrary",), vmem_limit_bytes=VMEM_LIMIT),
        name="experts",
    )(bexp, nact, xs, w1, b1p, w2, b2)


def _combine_kernel(pos_ref, y_ref, x1_ref, gate_ref, fg_ref, o_ref, ybuf_ref, sem):
    tr = x1_ref.shape[0]

    def issue(t, carry):
        for k in range(TOP_K):
            _row_copy(y_ref, pos_ref[k, t], ybuf_ref, k * tr + t, sem).start(priority=k % 2)
        return carry

    lax.fori_loop(0, tr, issue, 0, unroll=8)
    pltpu.make_async_copy(y_ref.at[pl.ds(0, TOP_K * tr), :], ybuf_ref, sem).wait()

    x = x1_ref[...]
    gates = gate_ref[...]
    for k in range(TOP_K):
        x = x + gates[:, k:k + 1] * ybuf_ref[k * tr:(k + 1) * tr, :]
    o_ref[...] = _rms(x, fg_ref[...])


def _combine(pos3, y, x1, gate_col, fg):
    n, d = x1.shape
    tr = pos3.shape[2]
    return pl.pallas_call(
        _combine_kernel,
        grid=(n // tr,),
        in_specs=[
            pl.BlockSpec((None, TOP_K, tr), lambda i: (i, 0, 0), memory_space=pltpu.SMEM),
            pl.BlockSpec(memory_space=pl.ANY),
            pl.BlockSpec((tr, d), lambda i: (i, 0)),
            pl.BlockSpec((tr, TOP_K), lambda i: (i, 0)),
            pl.BlockSpec((1, d), lambda i: (0, 0)),
        ],
        out_specs=pl.BlockSpec((tr, d), lambda i: (i, 0)),
        out_shape=jax.ShapeDtypeStruct((n, d), F32),
        scratch_shapes=[pltpu.VMEM((TOP_K * tr, d), F32), pltpu.SemaphoreType.DMA],
        compiler_params=pltpu.CompilerParams(
            dimension_semantics=("arbitrary",), vmem_limit_bytes=VMEM_LIMIT),
        name="combine",
    )(pos3, y, x1, gate_col, fg)


def _layer(x2, batch, seq, layer, lam_init, rel_bias_table, mix_norm_g, w_in, sgu_ln_g, sgu_ln_b, sgu_w,
           sgu_b, gmlp_out_g, lambda_q1, lambda_k1, lambda_q2, lambda_k2, diff_subln_g, w_out,
           ffn_norm_g, router_w, router_b, w_mlp1, b_mlp1, w_mlp2, b_mlp2):
    n, d = x2.shape
    l = layer
    row = lambda a: a[l].reshape(1, -1).astype(F32)

    gout, qt, k, vt = _mix_in(x2, row(mix_norm_g), w_in[l].astype(BF16), row(sgu_ln_g), row(sgu_ln_b),
                              sgu_w[l].astype(F32), sgu_b[l].astype(F32)[:, :, None], row(gmlp_out_g))
    dout = _diff_attn(qt, k, vt, _near_bias_tiles(rel_bias_table),
                      (row(lambda_q1), row(lambda_k1), row(lambda_q2), row(lambda_k2)),
                      row(diff_subln_g), batch, seq, lam_init)
    x1, h2, top_e, gate, rank, counts = _mix_out(
        x2, gout, dout, w_out[l].astype(BF16), row(ffn_norm_g),
        router_w[l].astype(F32).T, router_b[l].astype(F32)[:, None])

    counts = counts[:, 0]
    padded = (counts + MOE_BLOCK - 1) // MOE_BLOCK * MOE_BLOCK
    pad_end = jnp.cumsum(padded)
    pad_start = pad_end - padded
    nblk = n * TOP_K // MOE_BLOCK + N_EXPERTS
    cap = nblk * MOE_BLOCK
    block_start = jnp.arange(nblk, dtype=jnp.int32) * MOE_BLOCK
    block_exp = jnp.minimum(jnp.sum(pad_end[None, :] <= block_start[:, None], axis=1),
                            N_EXPERTS - 1).astype(jnp.int32)
    n_active = (pad_end[-1] // MOE_BLOCK).astype(jnp.int32).reshape(1)
    eids = jnp.arange(N_EXPERTS, dtype=jnp.int32)[:, None, None]
    pos = jnp.sum(jnp.where(top_e[None] == eids, pad_start[:, None, None], 0), axis=0) + rank
    tr = min(ROW_TILE, n)
    pos3 = pos.reshape(TOP_K, n // tr, tr).transpose(1, 0, 2)

    xs = _dispatch(counts, pad_start, padded, pos3, h2, cap)
    half = MXU_WIDTH // 2
    b1p = b_mlp1[l].astype(F32).reshape(N_EXPERTS, -1, half, 2).transpose(0, 1, 3, 2).reshape(N_EXPERTS, 1, -1)
    y = _experts(block_exp, n_active, xs, w_mlp1[l].astype(F32), b1p,
                 w_mlp2[l].astype(F32), b_mlp2[l][:, None, :].astype(F32))
    return pos3, y, x1, gate.T


def kernel(x, rel_bias_table, mix_norm_g, w_in, sgu_ln_g, sgu_ln_b, sgu_w, sgu_b, gmlp_out_g, lambda_q1,
           lambda_k1, lambda_q2, lambda_k2, diff_subln_g, w_out, ffn_norm_g, router_w, router_b, w_mlp1,
           b_mlp1, w_mlp2, b_mlp2, final_norm_g):
    batch, seq, d = x.shape
    depth = w_in.shape[0]
    assert depth == 1, "the final RMSNorm is fused into the last layer's combine step"
    assert seq % TOKEN_TILE == 0 or batch * seq <= TOKEN_TILE
    x2 = x.reshape(batch * seq, d).astype(F32)
    lam_init = 0.8 - 0.6 * math.exp(-0.3 * 0)
    pos3, y, x1, gate_col = _layer(
        x2, batch, seq, 0, lam_init, rel_bias_table, mix_norm_g, w_in, sgu_ln_g, sgu_ln_b, sgu_w, sgu_b,
        gmlp_out_g, lambda_q1, lambda_k1, lambda_q2, lambda_k2, diff_subln_g, w_out, ffn_norm_g,
        router_w, router_b, w_mlp1, b_mlp1, w_mlp2, b_mlp2)
    out = _combine(pos3, y, x1, gate_col, final_norm_g.reshape(1, d).astype(F32))
    return out.reshape(batch, seq, d).astype(x.dtype)
```

```python
import functools
import math

import jax
import jax.numpy as jnp
from jax import lax
from jax.experimental import pallas as pl
from jax.experimental.pallas import tpu as pltpu

F32 = jnp.float32
BF16 = jnp.bfloat16

CHUNK = 64
GMLP_GROUPS = 4
SGU_BLOCK = 128
DIFF_HEADS = 4
DIFF_QK_DIM = 64
DIFF_V_DIM = 128
REL_BUCKETS = 32
REL_MAX_DIST = 128
N_EXPERTS = 32
TOP_K = 4
SWIGLU_ALPHA = 1.702
SWIGLU_LIMIT = 7.0
NORM_EPS = 1e-5
NEG_INF = -1e30

LANES = 128
MXU_WIDTH = 256
ATT_TILE = 512
LOG2E = 1.4426950408889634
TOKEN_TILE = 512
MOE_BLOCK = 256
ROW_TILE = 256
VMEM_LIMIT = 56 * 1024 * 1024


def _rms(x, g):
    return x * lax.rsqrt(jnp.mean(x * x, axis=-1, keepdims=True) + NORM_EPS) * g


def _gelu(z):
    return 0.5 * z * (1.0 + lax.erf(z * (1.0 / math.sqrt(2.0))))


def _mix_in_kernel(x_ref, ng_ref, win_ref, lng_ref, lnb_ref, sw_ref, sb_ref, og_ref,
                   gout_ref, qt_ref, k_ref, vt_ref):
    tm = x_ref.shape[0]
    dg = gout_ref.shape[1]
    gw = dg // GMLP_GROUPS
    hb = _rms(x_ref[...], ng_ref[...]).astype(BF16)

    z = _gelu(jnp.dot(hb, win_ref[:, :2 * dg], preferred_element_type=F32))
    u = z[:, :dg]
    v = z[:, dg:]
    mu = jnp.mean(v, axis=-1, keepdims=True)
    vc = v - mu
    v = vc * lax.rsqrt(jnp.mean(vc * vc, axis=-1, keepdims=True) + NORM_EPS) * lng_ref[...] + lnb_ref[...]
    vb = v.astype(BF16)

    pos_i = lax.broadcasted_iota(jnp.int32, (SGU_BLOCK, SGU_BLOCK), 0) // CHUNK
    pos_j = lax.broadcasted_iota(jnp.int32, (SGU_BLOCK, SGU_BLOCK), 1) // CHUNK
    causal = pos_j <= pos_i
    rows = []
    for n in range(tm // SGU_BLOCK):
        cols = []
        for g in range(GMLP_GROUPS):
            w = jnp.where(causal, sw_ref[g], 0.0).astype(BF16)
            blk = vb[n * SGU_BLOCK:(n + 1) * SGU_BLOCK, g * gw:(g + 1) * gw]
            cols.append(jnp.dot(w, blk, preferred_element_type=F32) + sb_ref[g])
        rows.append(jnp.concatenate(cols, axis=1))
    sv = jnp.concatenate(rows, axis=0)
    gout_ref[...] = _rms(u * sv, og_ref[...]).astype(BF16)

    dd = DIFF_HEADS * LANES
    p = jnp.dot(hb, win_ref[:, 2 * dg:], preferred_element_type=F32)
    scale = DIFF_QK_DIM ** -0.5 * LOG2E
    for h in range(DIFF_HEADS):
        q = p[:, h * LANES:(h + 1) * LANES] * scale
        k = p[:, dd + h * LANES:dd + (h + 1) * LANES]
        vv = p[:, 2 * dd + h * LANES:2 * dd + (h + 1) * LANES]
        k_ref[h] = k.astype(BF16)
        qt = q.T.astype(BF16)
        vt = vv.T.astype(BF16)
        for t in range(tm // ATT_TILE):
            qt_ref[h, t] = qt[:, t * ATT_TILE:(t + 1) * ATT_TILE]
            vt_ref[h, t] = vt[:, t * ATT_TILE:(t + 1) * ATT_TILE]


def _mix_in(x2, ng, win, lng, lnb, sw, sb, og):
    n, d = x2.shape
    dg = lng.shape[1]
    tm = min(TOKEN_TILE, n)
    nt = tm // ATT_TILE
    const = lambda *shape: pl.BlockSpec(shape, lambda i: (0,) * len(shape))
    return pl.pallas_call(
        _mix_in_kernel,
        grid=(n // tm,),
        in_specs=[
            pl.BlockSpec((tm, d), lambda i: (i, 0)),
            const(1, d),
            const(*win.shape),
            const(1, dg),
            const(1, dg),
            const(*sw.shape),
            const(*sb.shape),
            const(1, dg),
        ],
        out_specs=[
            pl.BlockSpec((tm, dg), lambda i: (i, 0)),
            pl.BlockSpec((DIFF_HEADS, nt, LANES, ATT_TILE), lambda i: (0, i, 0, 0)),
            pl.BlockSpec((DIFF_HEADS, tm, LANES), lambda i: (0, i, 0)),
            pl.BlockSpec((DIFF_HEADS, nt, LANES, ATT_TILE), lambda i: (0, i, 0, 0)),
        ],
        out_shape=[
            jax.ShapeDtypeStruct((n, dg), BF16),
            jax.ShapeDtypeStruct((DIFF_HEADS, n // ATT_TILE, LANES, ATT_TILE), BF16),
            jax.ShapeDtypeStruct((DIFF_HEADS, n, LANES), BF16),
            jax.ShapeDtypeStruct((DIFF_HEADS, n // ATT_TILE, LANES, ATT_TILE), BF16),
        ],
        compiler_params=pltpu.CompilerParams(
            dimension_semantics=("arbitrary",), vmem_limit_bytes=VMEM_LIMIT),
        name="mix_in",
    )(x2, ng, win, lng, lnb, sw, sb, og)


def _attn_kernel(lq1_ref, lk1_ref, lq2_ref, lk2_ref, qt_ref, k_ref, vt_ref, bias_ref, g_ref,
                 o_ref, acc_ref, m_ref, l_ref, qw_ref, s0_ref, s1_ref, cm0_ref, cm1_ref, *, lam_init):
    qi = pl.program_id(2)
    t = ATT_TILE
    qt = qt_ref[...]
    drow = lax.broadcasted_iota(jnp.int32, qt.shape, 0)
    zero = jnp.zeros_like(qt)
    qw_ref[:, :t] = jnp.where(drow < DIFF_QK_DIM, qt, zero)
    qw_ref[:, t:] = jnp.where(drow >= DIFF_QK_DIM, qt, zero)

    acc_ref[...] = jnp.zeros_like(acc_ref)
    m_ref[...] = jnp.full_like(m_ref, NEG_INF)
    l_ref[...] = jnp.zeros_like(l_ref)
    bufs = ((s0_ref, cm0_ref), (s1_ref, cm1_ref))

    def produce(j, buf, kind):
        s_ref, cm_ref = buf
        k = k_ref[pl.ds(pl.multiple_of(j * t, t), t), :]
        s = jnp.dot(k, qw_ref[...], preferred_element_type=F32)
        if kind != "far":
            bias = bias_ref[0] if kind == "diag" else bias_ref[1]
            halves = [s[:, :t] + bias, s[:, t:] + bias]
            if kind == "diag":
                kc = lax.broadcasted_iota(jnp.int32, (t, t), 0) // CHUNK
                qc = lax.broadcasted_iota(jnp.int32, (t, t), 1) // CHUNK
                halves = [jnp.where(kc <= qc, h, NEG_INF) for h in halves]
            s = jnp.concatenate(halves, axis=1)
        s_ref[...] = s
        cm_ref[...] = jnp.max(s, axis=0, keepdims=True)

    def consume(j, buf):
        s_ref, cm_ref = buf
        m_old = m_ref[...]
        m_new = jnp.maximum(m_old, cm_ref[...])
        alpha = jnp.exp2(m_old - m_new)
        p = jnp.exp2(s_ref[...] - m_new)
        l_ref[...] = alpha * l_ref[...] + jnp.sum(p, axis=0, keepdims=True)
        m_ref[...] = m_new
        acc_ref[...] = alpha * acc_ref[...] + jnp.dot(vt_ref[j], p.astype(BF16), preferred_element_type=F32)

    def step(j, cbuf, pbuf, kind):
        produce(j + 1, pbuf, kind)
        consume(j, cbuf)

    n_far = jnp.maximum(qi - 2, 0)
    odd = n_far & 1

    @pl.when(qi == 0)
    def _():
        produce(0, bufs[0], "diag")

    @pl.when(qi == 1)
    def _():
        produce(0, bufs[1], "sub")

    @pl.when(jnp.logical_and(qi >= 2, odd == 0))
    def _():
        produce(0, bufs[0], "far")

    @pl.when(jnp.logical_and(qi >= 2, odd == 1))
    def _():
        produce(0, bufs[1], "far")

    @pl.when(odd == 1)
    def _():
        step(0, bufs[1], bufs[0], "far")

    def far_pair(i, carry):
        j = odd + 2 * i
        step(j, bufs[0], bufs[1], "far")
        step(j + 1, bufs[1], bufs[0], "far")
        return carry

    lax.fori_loop(0, lax.shift_right_logical(n_far, 1), far_pair, 0)

    @pl.when(qi >= 2)
    def _():
        step(qi - 2, bufs[0], bufs[1], "sub")

    @pl.when(qi >= 1)
    def _():
        step(qi - 1, bufs[1], bufs[0], "diag")

    consume(qi, bufs[0])

    lam = (jnp.exp(jnp.sum(lq1_ref[...] * lk1_ref[...], keepdims=True))
           - jnp.exp(jnp.sum(lq2_ref[...] * lk2_ref[...], keepdims=True)) + lam_init)
    on = acc_ref[...] * (1.0 / l_ref[...])
    o = on[:, :t] - lam * on[:, t:]
    o = o * lax.rsqrt(jnp.mean(o * o, axis=0, keepdims=True) + NORM_EPS)
    o_ref[...] = (o.T * (g_ref[...] * (1.0 - lam_init))).astype(BF16)


def _diff_attn(qt, k, vt, bias, lams, subln_g, batch, seq, lam_init):
    t = ATT_TILE
    nq = seq // t
    lam_spec = pl.BlockSpec((1, DIFF_QK_DIM), lambda b, h, i: (0, 0))
    return pl.pallas_call(
        functools.partial(_attn_kernel, lam_init=lam_init),
        grid=(batch, DIFF_HEADS, nq),
        in_specs=[
            lam_spec, lam_spec, lam_spec, lam_spec,
            pl.BlockSpec((None, None, LANES, t), lambda b, h, i: (h, b * nq + i, 0, 0)),
            pl.BlockSpec((None, seq, LANES), lambda b, h, i: (h, b, 0)),
            pl.BlockSpec((None, nq, LANES, t), lambda b, h, i: (h, b, 0, 0)),
            pl.BlockSpec((None, 2, t, t), lambda b, h, i: (h, 0, 0, 0)),
            pl.BlockSpec((1, DIFF_V_DIM), lambda b, h, i: (0, 0)),
        ],
        out_specs=pl.BlockSpec((t, DIFF_V_DIM), lambda b, h, i: (b * nq + i, h)),
        out_shape=jax.ShapeDtypeStruct((batch * seq, DIFF_HEADS * DIFF_V_DIM), BF16),
        scratch_shapes=[
            pltpu.VMEM((DIFF_V_DIM, 2 * t), F32),
            pltpu.VMEM((1, 2 * t), F32),
            pltpu.VMEM((1, 2 * t), F32),
            pltpu.VMEM((LANES, 2 * t), BF16),
            pltpu.VMEM((t, 2 * t), F32),
            pltpu.VMEM((t, 2 * t), F32),
            pltpu.VMEM((1, 2 * t), F32),
            pltpu.VMEM((1, 2 * t), F32),
        ],
        compiler_params=pltpu.CompilerParams(
            dimension_semantics=("arbitrary", "arbitrary", "arbitrary"), vmem_limit_bytes=VMEM_LIMIT),
        name="diff_attn",
    )(*lams, qt, k, vt, bias, subln_g)


def _rel_bucket(rel):
    nb = REL_BUCKETS // 2
    max_exact = nb // 2
    side = jnp.where(rel > 0, nb, 0)
    n = jnp.abs(rel)
    nf = jnp.maximum(n, 1).astype(F32)
    large = max_exact + (jnp.log(nf / max_exact) / math.log(REL_MAX_DIST / max_exact)
                         * (nb - max_exact)).astype(jnp.int32)
    large = jnp.minimum(large, nb - 1)
    return side + jnp.where(n < max_exact, n, large)


def _near_bias_tiles(rel_table):
    t = ATT_TILE
    kpos = jnp.arange(t)[:, None]
    qpos = jnp.arange(t)[None, :]
    rel = jnp.stack([kpos - qpos, kpos - qpos - t])
    onehot = _rel_bucket(rel)[..., None] == jnp.arange(REL_BUCKETS)
    table = rel_table.astype(F32)
    far = table[REL_BUCKETS // 2 - 1]
    tiles = [jnp.sum(jnp.where(onehot, table[:, h], 0.0), axis=-1) - far[h] for h in range(DIFF_HEADS)]
    return jnp.stack(tiles) * LOG2E


def _mix_out_kernel(x_ref, g_ref, d_ref, wo_ref, ng_ref, rwt_ref, rb_ref,
                    x1_ref, h2_ref, e_ref, gate_ref, rank_ref, cnt_ref, base_ref):
    i = pl.program_id(0)
    tm = x_ref.shape[0]
    dg = g_ref.shape[1]

    @pl.when(i == 0)
    def _():
        base_ref[...] = jnp.zeros_like(base_ref)

    mix = (jnp.dot(g_ref[...], wo_ref[:dg, :], preferred_element_type=F32)
           + jnp.dot(d_ref[...], wo_ref[dg:, :], preferred_element_type=F32))
    x1 = x_ref[...] + mix
    x1_ref[...] = x1
    h2 = _rms(x1, ng_ref[...])
    h2_ref[...] = h2

    logits = lax.dot_general(rwt_ref[...], h2, (((1,), (1,)), ((), ())),
                             precision=lax.Precision.HIGHEST,
                             preferred_element_type=F32) + rb_ref[...]
    eidx = lax.broadcasted_iota(jnp.int32, logits.shape, 0)
    cur = logits
    vals, sels = [], []
    for k in range(TOP_K):
        mx = jnp.max(cur, axis=0, keepdims=True)
        idx = jnp.min(jnp.where(cur == mx, eidx, N_EXPERTS), axis=0, keepdims=True)
        sel = eidx == idx
        vals.append(mx)
        sels.append(sel)
        e_ref[k:k + 1, :] = idx
        cur = jnp.where(sel, -jnp.inf, cur)
    ex = [jnp.exp(v - vals[0]) for v in vals]
    inv = 1.0 / (ex[0] + ex[1] + ex[2] + ex[3])
    for k in range(TOP_K):
        gate_ref[k:k + 1, :] = ex[k] * inv

    member = sels[0] | sels[1] | sels[2] | sels[3]
    mem_b = jnp.where(member, 1.0, 0.0).astype(BF16)
    ri = lax.broadcasted_iota(jnp.int32, (tm, tm), 0)
    ci = lax.broadcasted_iota(jnp.int32, (tm, tm), 1)
    tri = jnp.where(ri < ci, 1.0, 0.0).astype(BF16)
    excl = jnp.dot(mem_b, tri, preferred_element_type=F32)
    rank = base_ref[...] + excl
    for k in range(TOP_K):
        rank_ref[k:k + 1, :] = jnp.sum(jnp.where(sels[k], rank, 0.0), axis=0, keepdims=True).astype(jnp.int32)
    base_new = base_ref[...] + jnp.sum(mem_b.astype(F32), axis=1, keepdims=True)
    base_ref[...] = base_new
    cnt_ref[...] = jnp.broadcast_to(base_new, cnt_ref.shape).astype(jnp.int32)


def _mix_out(x2, gout, dout, wo, ng, rwt, rb):
    n, d = x2.shape
    dg = gout.shape[1]
    tm = min(TOKEN_TILE, n)
    const = lambda *shape: pl.BlockSpec(shape, lambda i: (0,) * len(shape))
    tok = lambda w: pl.BlockSpec((tm, w), lambda i: (i, 0))
    slot = pl.BlockSpec((TOP_K, tm), lambda i: (0, i))
    return pl.pallas_call(
        _mix_out_kernel,
        grid=(n // tm,),
        in_specs=[tok(d), tok(dg), tok(dout.shape[1]), const(*wo.shape), const(1, d),
                  const(*rwt.shape), const(N_EXPERTS, 1)],
        out_specs=[tok(d), tok(d), slot, slot, slot, const(N_EXPERTS, LANES)],
        out_shape=[
            jax.ShapeDtypeStruct((n, d), F32),
            jax.ShapeDtypeStruct((n, d), F32),
            jax.ShapeDtypeStruct((TOP_K, n), jnp.int32),
            jax.ShapeDtypeStruct((TOP_K, n), F32),
            jax.ShapeDtypeStruct((TOP_K, n), jnp.int32),
            jax.ShapeDtypeStruct((N_EXPERTS, LANES), jnp.int32),
        ],
        scratch_shapes=[pltpu.VMEM((N_EXPERTS, 1), F32)],
        compiler_params=pltpu.CompilerParams(
            dimension_semantics=("arbitrary",), vmem_limit_bytes=VMEM_LIMIT),
        name="mix_out",
    )(x2, gout, dout, wo, ng, rwt, rb)


def _row_copy(src_ref, src_row, dst_ref, dst_row, sem):
    return pltpu.make_async_copy(src_ref.at[pl.ds(src_row, 1), :], dst_ref.at[pl.ds(dst_row, 1), :], sem)


def _dispatch_kernel(cnt_ref, pstart_ref, padded_ref, pos_ref, h_ref, xs_ref, zrow_ref, sem, zsem):
    tr = h_ref.shape[0]

    @pl.when(pl.program_id(0) == 0)
    def _():
        zrow_ref[...] = jnp.zeros_like(zrow_ref)

        def pad_rows(start):
            def per_expert(e, carry):
                def per_row(r, c):
                    cp = _row_copy(zrow_ref, 0, xs_ref, pstart_ref[e] + r, zsem)
                    cp.start() if start else cp.wait()
                    return c
                lax.fori_loop(cnt_ref[e], padded_ref[e], per_row, 0)
                return carry
            lax.fori_loop(0, N_EXPERTS, per_expert, 0)

            zr = zrow_ref.shape[0]
            used = pstart_ref[N_EXPERTS - 1] + padded_ref[N_EXPERTS - 1]
            def per_tail(r, c):
                cp = pltpu.make_async_copy(
                    zrow_ref, xs_ref.at[pl.ds(pl.multiple_of(used + r * zr, zr), zr), :], zsem)
                cp.start() if start else cp.wait()
                return c
            lax.fori_loop(0, (xs_ref.shape[0] - used) // zr, per_tail, 0)

        pad_rows(True)
        pad_rows(False)

    def issue(t, carry):
        for k in range(TOP_K):
            _row_copy(h_ref, t, xs_ref, pos_ref[k, t], sem).start(priority=k % 2)
        return carry

    lax.fori_loop(0, tr, issue, 0, unroll=8)
    for k in range(TOP_K):
        pltpu.make_async_copy(h_ref, xs_ref.at[pl.ds(0, tr), :], sem).wait()


def _dispatch(counts, pad_start, padded, pos3, h2, cap):
    n, d = h2.shape
    tr = pos3.shape[2]
    return pl.pallas_call(
        _dispatch_kernel,
        grid_spec=pltpu.PrefetchScalarGridSpec(
            num_scalar_prefetch=3,
            grid=(n // tr,),
            in_specs=[
                pl.BlockSpec((None, TOP_K, tr), lambda i, *_: (i, 0, 0), memory_space=pltpu.SMEM),
                pl.BlockSpec((tr, d), lambda i, *_: (i, 0)),
            ],
            out_specs=pl.BlockSpec(memory_space=pl.ANY),
            scratch_shapes=[pltpu.VMEM((8, d), F32), pltpu.SemaphoreType.DMA, pltpu.SemaphoreType.DMA],
        ),
        out_shape=jax.ShapeDtypeStruct((cap, d), F32),
        compiler_params=pltpu.CompilerParams(dimension_semantics=("arbitrary",)),
        name="dispatch",
    )(counts, pad_start, padded, pos3, h2)


def _expert_kernel(bexp_ref, nact_ref, xs_ref, w1_ref, b1_ref, w2_ref, b2_ref, y_ref, w1p_ref, w2b_ref):
    b = pl.program_id(0)
    active = b < nact_ref[0]
    fresh = jnp.logical_or(b == 0, bexp_ref[b] != bexp_ref[jnp.maximum(b - 1, 0)])
    gw = MXU_WIDTH
    half = gw // 2

    @pl.when(jnp.logical_not(active))
    def _():
        y_ref[...] = jnp.zeros_like(y_ref)

    @pl.when(jnp.logical_and(active, fresh))
    def _():
        r = lax.broadcasted_iota(jnp.int32, (gw, gw), 0)
        c = lax.broadcasted_iota(jnp.int32, (gw, gw), 1)
        perm = jnp.where(r == jnp.where(c < half, 2 * c, 2 * (c - half) + 1), 1.0, 0.0).astype(BF16)
        for j in range(w1_ref.shape[1] // gw):
            w = w1_ref[:, j * gw:(j + 1) * gw].astype(BF16)
            w1p_ref[:, j * gw:(j + 1) * gw] = jnp.dot(w, perm, preferred_element_type=F32).astype(BF16)
        w2b_ref[...] = w2_ref[...].astype(BF16)

    @pl.when(active)
    def _():
        x = xs_ref[...].astype(BF16)
        a = jnp.dot(x, w1p_ref[...], preferred_element_type=F32) + b1_ref[...]
        acts = []
        for j in range(a.shape[1] // gw):
            ag = jnp.minimum(a[:, j * gw:j * gw + half], SWIGLU_LIMIT)
            al = jnp.clip(a[:, j * gw + half:(j + 1) * gw], -SWIGLU_LIMIT, SWIGLU_LIMIT)
            acts.append((ag * (1.0 / (1.0 + jnp.exp(-SWIGLU_ALPHA * ag))) * (al + 1.0)).astype(BF16))
        act = jnp.concatenate(acts, axis=1)
        y_ref[...] = jnp.dot(act, w2b_ref[...], preferred_element_type=F32) + b2_ref[...]


def _experts(bexp, nact, xs, w1, b1p, w2, b2):
    cap, d = xs.shape
    de2 = w1.shape[2]
    de = w2.shape[1]
    nblk = cap // MOE_BLOCK
    row = lambda b, be, na: (jnp.minimum(b, na[0] - 1), 0)
    exp3 = lambda b, be, na: (be[b], 0, 0)
    return pl.pallas_call(
        _expert_kernel,
        grid_spec=pltpu.PrefetchScalarGridSpec(
            num_scalar_prefetch=2,
            grid=(nblk,),
            in_specs=[
                pl.BlockSpec((MOE_BLOCK, d), row),
                pl.BlockSpec((None, d, de2), exp3),
                pl.BlockSpec((None, 1, de2), exp3),
                pl.BlockSpec((None, de, d), exp3),
                pl.BlockSpec((None, 1, d), exp3),
            ],
            out_specs=pl.BlockSpec((MOE_BLOCK, d), lambda b, be, na: (b, 0)),
            scratch_shapes=[pltpu.VMEM((d, de2), BF16), pltpu.VMEM((de, d), BF16)],
        ),
        out_shape=jax.ShapeDtypeStruct((cap, d), F32),
        compiler_params=pltpu.CompilerParams(
            dimension_semantics=("arbitrary",), vmem_limit_bytes=VMEM_LIMIT),
        name="experts",
    )(bexp, nact, xs, w1, b1p, w2, b2)


def _combine_kernel(pos_ref, y_ref, x1_ref, gate_ref, fg_ref, o_ref, ybuf_ref, sem):
    tr = x1_ref.shape[0]

    def issue(t, carry):
        for k in range(TOP_K):
            _row_copy(y_ref, pos_ref[k, t], ybuf_ref, k * tr + t, sem).start(priority=k % 2)
        return carry

    lax.fori_loop(0, tr, issue, 0, unroll=8)
    pltpu.make_async_copy(y_ref.at[pl.ds(0, TOP_K * tr), :], ybuf_ref, sem).wait()

    x = x1_ref[...]
    gates = gate_ref[...]
    for k in range(TOP_K):
        x = x + gates[:, k:k + 1] * ybuf_ref[k * tr:(k + 1) * tr, :]
    o_ref[...] = _rms(x, fg_ref[...])


def _combine(pos3, y, x1, gate_col, fg):
    n, d = x1.shape
    tr = pos3.shape[2]
    return pl.pallas_call(
        _combine_kernel,
        grid=(n // tr,),
        in_specs=[
            pl.BlockSpec((None, TOP_K, tr), lambda i: (i, 0, 0), memory_space=pltpu.SMEM),
            pl.BlockSpec(memory_space=pl.ANY),
            pl.BlockSpec((tr, d), lambda i: (i, 0)),
            pl.BlockSpec((tr, TOP_K), lambda i: (i, 0)),
            pl.BlockSpec((1, d), lambda i: (0, 0)),
        ],
        out_specs=pl.BlockSpec((tr, d), lambda i: (i, 0)),
        out_shape=jax.ShapeDtypeStruct((n, d), F32),
        scratch_shapes=[pltpu.VMEM((TOP_K * tr, d), F32), pltpu.SemaphoreType.DMA],
        compiler_params=pltpu.CompilerParams(
            dimension_semantics=("arbitrary",), vmem_limit_bytes=VMEM_LIMIT),
        name="combine",
    )(pos3, y, x1, gate_col, fg)


def _layer(x2, batch, seq, layer, lam_init, rel_bias_table, mix_norm_g, w_in, sgu_ln_g, sgu_ln_b, sgu_w,
           sgu_b, gmlp_out_g, lambda_q1, lambda_k1, lambda_q2, lambda_k2, diff_subln_g, w_out,
           ffn_norm_g, router_w, router_b, w_mlp1, b_mlp1, w_mlp2, b_mlp2):
    n, d = x2.shape
    l = layer
    row = lambda a: a[l].reshape(1, -1).astype(F32)

    gout, qt, k, vt = _mix_in(x2, row(mix_norm_g), w_in[l].astype(BF16), row(sgu_ln_g), row(sgu_ln_b),
                              sgu_w[l].astype(F32), sgu_b[l].astype(F32)[:, :, None], row(gmlp_out_g))
    dout = _diff_attn(qt, k, vt, _near_bias_tiles(rel_bias_table),
                      (row(lambda_q1), row(lambda_k1), row(lambda_q2), row(lambda_k2)),
                      row(diff_subln_g), batch, seq, lam_init)
    x1, h2, top_e, gate, rank, counts = _mix_out(
        x2, gout, dout, w_out[l].astype(BF16), row(ffn_norm_g),
        router_w[l].astype(F32).T, router_b[l].astype(F32)[:, None])

    counts = counts[:, 0]
    padded = (counts + MOE_BLOCK - 1) // MOE_BLOCK * MOE_BLOCK
    pad_end = jnp.cumsum(padded)
    pad_start = pad_end - padded
    nblk = n * TOP_K // MOE_BLOCK + N_EXPERTS
    cap = nblk * MOE_BLOCK
    block_start = jnp.arange(nblk, dtype=jnp.int32) * MOE_BLOCK
    block_exp = jnp.minimum(jnp.sum(pad_end[None, :] <= block_start[:, None], axis=1),
                            N_EXPERTS - 1).astype(jnp.int32)
    n_active = (pad_end[-1] // MOE_BLOCK).astype(jnp.int32).reshape(1)
    eids = jnp.arange(N_EXPERTS, dtype=jnp.int32)[:, None, None]
    pos = jnp.sum(jnp.where(top_e[None] == eids, pad_start[:, None, None], 0), axis=0) + rank
    tr = min(ROW_TILE, n)
    pos3 = pos.reshape(TOP_K, n // tr, tr).transpose(1, 0, 2)

    xs = _dispatch(counts, pad_start, padded, pos3, h2, cap)
    half = MXU_WIDTH // 2
    b1p = b_mlp1[l].astype(F32).reshape(N_EXPERTS, -1, half, 2).transpose(0, 1, 3, 2).reshape(N_EXPERTS, 1, -1)
    y = _experts(block_exp, n_active, xs, w_mlp1[l].astype(F32), b1p,
                 w_mlp2[l].astype(F32), b_mlp2[l][:, None, :].astype(F32))
    return pos3, y, x1, gate.T


def kernel(x, rel_bias_table, mix_norm_g, w_in, sgu_ln_g, sgu_ln_b, sgu_w, sgu_b, gmlp_out_g, lambda_q1,
           lambda_k1, lambda_q2, lambda_k2, diff_subln_g, w_out, ffn_norm_g, router_w, router_b, w_mlp1,
           b_mlp1, w_mlp2, b_mlp2, final_norm_g):
    batch, seq, d = x.shape
    depth = w_in.shape[0]
    assert depth == 1, "the final RMSNorm is fused into the last layer's combine step"
    assert seq % TOKEN_TILE == 0 or batch * seq <= TOKEN_TILE
    x2 = x.reshape(batch * seq, d).astype(F32)
    lam_init = 0.8 - 0.6 * math.exp(-0.3 * 0)
    pos3, y, x1, gate_col = _layer(
        x2, batch, seq, 0, lam_init, rel_bias_table, mix_norm_g, w_in, sgu_ln_g, sgu_ln_b, sgu_w, sgu_b,
        gmlp_out_g, lambda_q1, lambda_k1, lambda_q2, lambda_k2, diff_subln_g, w_out, ffn_norm_g,
        router_w, router_b, w_mlp1, b_mlp1, w_mlp2, b_mlp2)
    out = _combine(pos3, y, x1, gate_col, final_norm_g.reshape(1, d).astype(F32))
    return out.reshape(batch, seq, d).astype(x.dtype)
```

```python
import functools
import math

import jax
import jax.numpy as jnp
from jax import lax
from jax.experimental import pallas as pl
from jax.experimental.pallas import tpu as pltpu

F32 = jnp.float32
BF16 = jnp.bfloat16

CHUNK = 64
GMLP_GROUPS = 4
SGU_BLOCK = 128
DIFF_HEADS = 4
DIFF_QK_DIM = 64
DIFF_V_DIM = 128
REL_BUCKETS = 32
REL_MAX_DIST = 128
N_EXPERTS = 32
TOP_K = 4
SWIGLU_ALPHA = 1.702
SWIGLU_LIMIT = 7.0
NORM_EPS = 1e-5
NEG_INF = -1e30

LANES = 128
SUBLANES = 8
MXU_WIDTH = 256
ATT_TILE = 512
LOG2E = 1.4426950408889634
TOKEN_TILE = 512
MOE_BLOCK = 512
VMEM_LIMIT = 56 * 1024 * 1024


def _rms(x, g):
    return x * lax.rsqrt(jnp.mean(x * x, axis=-1, keepdims=True) + NORM_EPS) * g


def _gelu(z):
    return 0.5 * z * (1.0 + lax.erf(z * (1.0 / math.sqrt(2.0))))


def _mix_in_kernel(x_ref, ng_ref, win_ref, lng_ref, lnb_ref, sw_ref, sb_ref, og_ref,
                   gout_ref, qt_ref, k_ref, vt_ref):
    tm = x_ref.shape[0]
    dg = gout_ref.shape[1]
    gw = dg // GMLP_GROUPS
    hb = _rms(x_ref[...], ng_ref[...]).astype(BF16)

    z = _gelu(jnp.dot(hb, win_ref[:, :2 * dg], preferred_element_type=F32))
    u = z[:, :dg]
    v = z[:, dg:]
    mu = jnp.mean(v, axis=-1, keepdims=True)
    vc = v - mu
    v = vc * lax.rsqrt(jnp.mean(vc * vc, axis=-1, keepdims=True) + NORM_EPS) * lng_ref[...] + lnb_ref[...]
    vb = v.astype(BF16)

    pos_i = lax.broadcasted_iota(jnp.int32, (SGU_BLOCK, SGU_BLOCK), 0) // CHUNK
    pos_j = lax.broadcasted_iota(jnp.int32, (SGU_BLOCK, SGU_BLOCK), 1) // CHUNK
    causal = pos_j <= pos_i
    rows = []
    for n in range(tm // SGU_BLOCK):
        cols = []
        for g in range(GMLP_GROUPS):
            w = jnp.where(causal, sw_ref[g], 0.0).astype(BF16)
            blk = vb[n * SGU_BLOCK:(n + 1) * SGU_BLOCK, g * gw:(g + 1) * gw]
            cols.append(jnp.dot(w, blk, preferred_element_type=F32) + sb_ref[g])
        rows.append(jnp.concatenate(cols, axis=1))
    sv = jnp.concatenate(rows, axis=0)
    gout_ref[...] = _rms(u * sv, og_ref[...]).astype(BF16)

    dd = DIFF_HEADS * LANES
    p = jnp.dot(hb, win_ref[:, 2 * dg:], preferred_element_type=F32)
    scale = DIFF_QK_DIM ** -0.5 * LOG2E
    for h in range(DIFF_HEADS):
        q = p[:, h * LANES:(h + 1) * LANES] * scale
        k = p[:, dd + h * LANES:dd + (h + 1) * LANES]
        vv = p[:, 2 * dd + h * LANES:2 * dd + (h + 1) * LANES]
        k_ref[h] = k.astype(BF16)
        qt = q.T.astype(BF16)
        vt = vv.T.astype(BF16)
        for t in range(tm // ATT_TILE):
            qt_ref[h, t] = qt[:, t * ATT_TILE:(t + 1) * ATT_TILE]
            vt_ref[h, t] = vt[:, t * ATT_TILE:(t + 1) * ATT_TILE]


def _mix_in(x2, ng, win, lng, lnb, sw, sb, og):
    n, d = x2.shape
    dg = lng.shape[1]
    tm = min(TOKEN_TILE, n)
    nt = tm // ATT_TILE
    const = lambda *shape: pl.BlockSpec(shape, lambda i: (0,) * len(shape))
    return pl.pallas_call(
        _mix_in_kernel,
        grid=(n // tm,),
        in_specs=[
            pl.BlockSpec((tm, d), lambda i: (i, 0)),
            const(1, d),
            const(*win.shape),
            const(1, dg),
            const(1, dg),
            const(*sw.shape),
            const(*sb.shape),
            const(1, dg),
        ],
        out_specs=[
            pl.BlockSpec((tm, dg), lambda i: (i, 0)),
            pl.BlockSpec((DIFF_HEADS, nt, LANES, ATT_TILE), lambda i: (0, i, 0, 0)),
            pl.BlockSpec((DIFF_HEADS, tm, LANES), lambda i: (0, i, 0)),
            pl.BlockSpec((DIFF_HEADS, nt, LANES, ATT_TILE), lambda i: (0, i, 0, 0)),
        ],
        out_shape=[
            jax.ShapeDtypeStruct((n, dg), BF16),
            jax.ShapeDtypeStruct((DIFF_HEADS, n // ATT_TILE, LANES, ATT_TILE), BF16),
            jax.ShapeDtypeStruct((DIFF_HEADS, n, LANES), BF16),
            jax.ShapeDtypeStruct((DIFF_HEADS, n // ATT_TILE, LANES, ATT_TILE), BF16),
        ],
        compiler_params=pltpu.CompilerParams(
            dimension_semantics=("arbitrary",), vmem_limit_bytes=VMEM_LIMIT),
        name="mix_in",
    )(x2, ng, win, lng, lnb, sw, sb, og)


def _attn_kernel(lq1_ref, lk1_ref, lq2_ref, lk2_ref, qt_ref, k_ref, vt_ref, bias_ref, g_ref,
                 o_ref, acc_ref, m_ref, l_ref, qw_ref, s0_ref, s1_ref, cm0_ref, cm1_ref, *, lam_init):
    qi = pl.program_id(2)
    t = ATT_TILE
    qt = qt_ref[...]
    drow = lax.broadcasted_iota(jnp.int32, qt.shape, 0)
    zero = jnp.zeros_like(qt)
    qw_ref[:, :t] = jnp.where(drow < DIFF_QK_DIM, qt, zero)
    qw_ref[:, t:] = jnp.where(drow >= DIFF_QK_DIM, qt, zero)

    acc_ref[...] = jnp.zeros_like(acc_ref)
    m_ref[...] = jnp.full_like(m_ref, NEG_INF)
    l_ref[...] = jnp.zeros_like(l_ref)
    bufs = ((s0_ref, cm0_ref), (s1_ref, cm1_ref))

    def produce(j, buf, kind):
        s_ref, cm_ref = buf
        k = k_ref[pl.ds(pl.multiple_of(j * t, t), t), :]
        s = jnp.dot(k, qw_ref[...], preferred_element_type=F32)
        if kind != "far":
            bias = bias_ref[0] if kind == "diag" else bias_ref[1]
            halves = [s[:, :t] + bias, s[:, t:] + bias]
            if kind == "diag":
                kc = lax.broadcasted_iota(jnp.int32, (t, t), 0) // CHUNK
                qc = lax.broadcasted_iota(jnp.int32, (t, t), 1) // CHUNK
                halves = [jnp.where(kc <= qc, h, NEG_INF) for h in halves]
            s = jnp.concatenate(halves, axis=1)
        s_ref[...] = s
        cm_ref[...] = jnp.max(s, axis=0, keepdims=True)

    def consume(j, buf):
        s_ref, cm_ref = buf
        m_old = m_ref[...]
        m_new = jnp.maximum(m_old, cm_ref[...])
        alpha = jnp.exp2(m_old - m_new)
        p = jnp.exp2(s_ref[...] - m_new)
        l_ref[...] = alpha * l_ref[...] + jnp.sum(p, axis=0, keepdims=True)
        m_ref[...] = m_new
        acc_ref[...] = alpha * acc_ref[...] + jnp.dot(vt_ref[j], p.astype(BF16), preferred_element_type=F32)

    def step(j, cbuf, pbuf, kind):
        produce(j + 1, pbuf, kind)
        consume(j, cbuf)

    n_far = jnp.maximum(qi - 2, 0)
    odd = n_far & 1

    @pl.when(qi == 0)
    def _():
        produce(0, bufs[0], "diag")

    @pl.when(qi == 1)
    def _():
        produce(0, bufs[1], "sub")

    @pl.when(jnp.logical_and(qi >= 2, odd == 0))
    def _():
        produce(0, bufs[0], "far")

    @pl.when(jnp.logical_and(qi >= 2, odd == 1))
    def _():
        produce(0, bufs[1], "far")

    @pl.when(odd == 1)
    def _():
        step(0, bufs[1], bufs[0], "far")

    def far_pair(i, carry):
        j = odd + 2 * i
        step(j, bufs[0], bufs[1], "far")
        step(j + 1, bufs[1], bufs[0], "far")
        return carry

    lax.fori_loop(0, lax.shift_right_logical(n_far, 1), far_pair, 0)

    @pl.when(qi >= 2)
    def _():
        step(qi - 2, bufs[0], bufs[1], "sub")

    @pl.when(qi >= 1)
    def _():
        step(qi - 1, bufs[1], bufs[0], "diag")

    consume(qi, bufs[0])

    lam = (jnp.exp(jnp.sum(lq1_ref[...] * lk1_ref[...], keepdims=True))
           - jnp.exp(jnp.sum(lq2_ref[...] * lk2_ref[...], keepdims=True)) + lam_init)
    on = acc_ref[...] * (1.0 / l_ref[...])
    o = on[:, :t] - lam * on[:, t:]
    o = o * lax.rsqrt(jnp.mean(o * o, axis=0, keepdims=True) + NORM_EPS)
    o_ref[...] = (o.T * (g_ref[...] * (1.0 - lam_init))).astype(BF16)


def _diff_attn(qt, k, vt, bias, lams, subln_g, batch, seq, lam_init):
    t = ATT_TILE
    nq = seq // t
    lam_spec = pl.BlockSpec((1, DIFF_QK_DIM), lambda b, h, i: (0, 0))
    return pl.pallas_call(
        functools.partial(_attn_kernel, lam_init=lam_init),
        grid=(batch, DIFF_HEADS, nq),
        in_specs=[
            lam_spec, lam_spec, lam_spec, lam_spec,
            pl.BlockSpec((None, None, LANES, t), lambda b, h, i: (h, b * nq + i, 0, 0)),
            pl.BlockSpec((None, seq, LANES), lambda b, h, i: (h, b, 0)),
            pl.BlockSpec((None, nq, LANES, t), lambda b, h, i: (h, b, 0, 0)),
            pl.BlockSpec((None, 2, t, t), lambda b, h, i: (h, 0, 0, 0)),
            pl.BlockSpec((1, DIFF_V_DIM), lambda b, h, i: (0, 0)),
        ],
        out_specs=pl.BlockSpec((t, DIFF_V_DIM), lambda b, h, i: (b * nq + i, h)),
        out_shape=jax.ShapeDtypeStruct((batch * seq, DIFF_HEADS * DIFF_V_DIM), BF16),
        scratch_shapes=[
            pltpu.VMEM((DIFF_V_DIM, 2 * t), F32),
            pltpu.VMEM((1, 2 * t), F32),
            pltpu.VMEM((1, 2 * t), F32),
            pltpu.VMEM((LANES, 2 * t), BF16),
            pltpu.VMEM((t, 2 * t), F32),
            pltpu.VMEM((t, 2 * t), F32),
            pltpu.VMEM((1, 2 * t), F32),
            pltpu.VMEM((1, 2 * t), F32),
        ],
        compiler_params=pltpu.CompilerParams(
            dimension_semantics=("arbitrary", "arbitrary", "arbitrary"), vmem_limit_bytes=VMEM_LIMIT),
        name="diff_attn",
    )(*lams, qt, k, vt, bias, subln_g)


def _rel_bucket(rel):
    nb = REL_BUCKETS // 2
    max_exact = nb // 2
    side = jnp.where(rel > 0, nb, 0)
    n = jnp.abs(rel)
    nf = jnp.maximum(n, 1).astype(F32)
    large = max_exact + (jnp.log(nf / max_exact) / math.log(REL_MAX_DIST / max_exact)
                         * (nb - max_exact)).astype(jnp.int32)
    large = jnp.minimum(large, nb - 1)
    return side + jnp.where(n < max_exact, n, large)


def _near_bias_tiles(rel_table):
    t = ATT_TILE
    kpos = jnp.arange(t)[:, None]
    qpos = jnp.arange(t)[None, :]
    rel = jnp.stack([kpos - qpos, kpos - qpos - t])
    onehot = _rel_bucket(rel)[..., None] == jnp.arange(REL_BUCKETS)
    table = rel_table.astype(F32)
    far = table[REL_BUCKETS // 2 - 1]
    tiles = [jnp.sum(jnp.where(onehot, table[:, h], 0.0), axis=-1) - far[h] for h in range(DIFF_HEADS)]
    return jnp.stack(tiles) * LOG2E


def _mix_out_kernel(x_ref, g_ref, d_ref, wo_ref, ng_ref, rwt_ref, rb_ref,
                    x1_ref, h2_ref, gate_ref, lpos_ref, lstart_ref, lcnt_ref, gbase_ref, cnt_ref, base_ref):
    i = pl.program_id(0)
    tm = x_ref.shape[0]
    dg = g_ref.shape[1]

    @pl.when(i == 0)
    def _():
        base_ref[...] = jnp.zeros_like(base_ref)

    mix = (jnp.dot(g_ref[...], wo_ref[:dg, :], preferred_element_type=F32)
           + jnp.dot(d_ref[...], wo_ref[dg:, :], preferred_element_type=F32))
    x1 = x_ref[...] + mix
    x1_ref[...] = x1
    h2 = _rms(x1, ng_ref[...])
    h2_ref[...] = h2.astype(BF16)

    logits = lax.dot_general(rwt_ref[...], h2, (((1,), (1,)), ((), ())),
                             precision=lax.Precision.HIGHEST,
                             preferred_element_type=F32) + rb_ref[...]
    eidx = lax.broadcasted_iota(jnp.int32, logits.shape, 0)
    cur = logits
    vals, sels = [], []
    for k in range(TOP_K):
        mx = jnp.max(cur, axis=0, keepdims=True)
        idx = jnp.min(jnp.where(cur == mx, eidx, N_EXPERTS), axis=0, keepdims=True)
        sel = eidx == idx
        vals.append(mx)
        sels.append(sel)
        cur = jnp.where(sel, -jnp.inf, cur)
    ex = [jnp.exp(v - vals[0]) for v in vals]
    inv = 1.0 / (ex[0] + ex[1] + ex[2] + ex[3])
    for k in range(TOP_K):
        gate_ref[k:k + 1, :] = ex[k] * inv

    member = sels[0] | sels[1] | sels[2] | sels[3]
    mem_b = jnp.where(member, 1.0, 0.0).astype(BF16)
    ri = lax.broadcasted_iota(jnp.int32, (tm, tm), 0)
    ci = lax.broadcasted_iota(jnp.int32, (tm, tm), 1)
    tri = jnp.where(ri < ci, 1.0, 0.0).astype(BF16)
    excl = jnp.dot(mem_b, tri, preferred_element_type=F32)
    er = lax.broadcasted_iota(jnp.int32, (N_EXPERTS, N_EXPERTS), 0)
    ec = lax.broadcasted_iota(jnp.int32, (N_EXPERTS, N_EXPERTS), 1)
    lower = jnp.where(ec < er, 1.0, 0.0).astype(BF16)
    lcnt = jnp.floor((jnp.sum(mem_b.astype(F32), axis=1, keepdims=True) + (SUBLANES - 1)) * (1.0 / SUBLANES))
    groups = jnp.broadcast_to(lcnt, (N_EXPERTS, LANES)).astype(BF16)
    lstart = jnp.dot(lower, groups, preferred_element_type=F32)[:, :1] * SUBLANES
    lcnt = lcnt * SUBLANES
    lrank = lstart + excl
    for k in range(TOP_K):
        lpos_ref[k:k + 1, :] = jnp.sum(jnp.where(sels[k], lrank, 0.0), axis=0, keepdims=True).astype(jnp.int32)
    gbase = base_ref[...]
    lstart_ref[...] = jnp.broadcast_to(lstart, lstart_ref.shape).astype(jnp.int32)
    lcnt_ref[...] = jnp.broadcast_to(lcnt, lcnt_ref.shape).astype(jnp.int32)
    gbase_ref[...] = jnp.broadcast_to(gbase, gbase_ref.shape).astype(jnp.int32)
    base_ref[...] = gbase + lcnt
    cnt_ref[...] = jnp.broadcast_to(gbase + lcnt, cnt_ref.shape).astype(jnp.int32)


def _mix_out(x2, gout, dout, wo, ng, rwt, rb):
    n, d = x2.shape
    dg = gout.shape[1]
    tm = min(TOKEN_TILE, n)
    nt = n // tm
    const = lambda *shape: pl.BlockSpec(shape, lambda i: (0,) * len(shape))
    tok = lambda w: pl.BlockSpec((tm, w), lambda i: (i, 0))
    slot = pl.BlockSpec((TOP_K, tm), lambda i: (0, i))
    seg = pl.BlockSpec((N_EXPERTS, LANES), lambda i: (i, 0))
    seg_shape = jax.ShapeDtypeStruct((nt * N_EXPERTS, LANES), jnp.int32)
    return pl.pallas_call(
        _mix_out_kernel,
        grid=(nt,),
        in_specs=[tok(d), tok(dg), tok(dout.shape[1]), const(*wo.shape), const(1, d),
                  const(*rwt.shape), const(N_EXPERTS, 1)],
        out_specs=[tok(d), tok(d), slot, slot, seg, seg, seg, const(N_EXPERTS, LANES)],
        out_shape=[
            jax.ShapeDtypeStruct((n, d), F32),
            jax.ShapeDtypeStruct((n, d), BF16),
            jax.ShapeDtypeStruct((TOP_K, n), F32),
            jax.ShapeDtypeStruct((TOP_K, n), jnp.int32),
            seg_shape, seg_shape, seg_shape,
            jax.ShapeDtypeStruct((N_EXPERTS, LANES), jnp.int32),
        ],
        scratch_shapes=[pltpu.VMEM((N_EXPERTS, 1), F32)],
        compiler_params=pltpu.CompilerParams(
            dimension_semantics=("arbitrary",), vmem_limit_bytes=VMEM_LIMIT),
        name="mix_out",
    )(x2, gout, dout, wo, ng, rwt, rb)


def _stage_rows(tm):
    return pl.cdiv(TOP_K * tm + N_EXPERTS * (SUBLANES - 1), MXU_WIDTH) * MXU_WIDTH


def _segment_copies(src_ref, src_row, dst_ref, dst_row, length, max_len, sem, start):
    size = SUBLANES
    while size <= max_len:
        off = length & (size - 1)

        @pl.when((length & size) != 0)
        def _(size=size, off=off):
            cp = pltpu.make_async_copy(src_ref.at[pl.ds(pl.multiple_of(src_row + off, SUBLANES), size), :],
                                       dst_ref.at[pl.ds(pl.multiple_of(dst_row + off, SUBLANES), size), :], sem)
            cp.start() if start else cp.wait()
        size *= 2


def _pack_bf16_pairs(x):
    c = x.shape[1] // 2
    xb = x.astype(BF16).astype(F32)
    lo = lax.shift_right_logical(lax.bitcast_convert_type(xb[:, :c], jnp.uint32), jnp.uint32(16))
    hi = lax.bitcast_convert_type(xb[:, c:], jnp.uint32) & jnp.uint32(0xFFFF0000)
    return hi | lo


def _unpack_bf16_pairs(w):
    lo = lax.bitcast_convert_type(lax.shift_left(w, jnp.uint32(16)), F32)
    hi = lax.bitcast_convert_type(w & jnp.uint32(0xFFFF0000), F32)
    return jnp.concatenate([lo, hi], axis=1).astype(BF16)


def _dispatch_kernel(cnt_ref, pstart_ref, padded_ref, ssrc_ref, slen_ref, sdst_ref,
                     lpos_ref, h_ref, xs_ref, stage_ref, zero_ref, sem, zsem):
    i = pl.program_id(0)
    tm = h_ref.shape[0]
    zr = zero_ref.shape[0]

    @pl.when(i == 0)
    def _():
        zero_ref[...] = jnp.zeros_like(zero_ref)
        used = pstart_ref[N_EXPERTS - 1] + padded_ref[N_EXPERTS - 1]

        def pad_rows(start):
            def per_expert(e, carry):
                _segment_copies(zero_ref, 0, xs_ref, pstart_ref[e] + cnt_ref[e], padded_ref[e] - cnt_ref[e],
                                zr, zsem, start)
                return carry
            lax.fori_loop(0, N_EXPERTS, per_expert, 0)

            def per_tail(r, carry):
                cp = pltpu.make_async_copy(
                    zero_ref, xs_ref.at[pl.ds(pl.multiple_of(used + r * zr, zr), zr), :], zsem)
                cp.start() if start else cp.wait()
                return carry
            lax.fori_loop(0, (xs_ref.shape[0] - used) // zr, per_tail, 0)

        pad_rows(True)
        pad_rows(False)

    h = h_ref[...]
    chunk = MXU_WIDTH
    lp = [lpos_ref[k:k + 1, :] for k in range(TOP_K)]
    for c in range(stage_ref.shape[0] // chunk):
        r = lax.broadcasted_iota(jnp.int32, (chunk, tm), 0) + c * chunk
        hit = (r == lp[0]) | (r == lp[1]) | (r == lp[2]) | (r == lp[3])
        perm = jnp.where(hit, 1.0, 0.0).astype(BF16)
        rows = jnp.dot(perm, h, preferred_element_type=F32)
        stage_ref[c * chunk:(c + 1) * chunk, :] = _pack_bf16_pairs(rows)

    def per_segment(e, carry, start):
        s = i * N_EXPERTS + e
        _segment_copies(stage_ref, ssrc_ref[s], xs_ref, sdst_ref[s], slen_ref[s], tm, sem, start)
        return carry

    for start in (True, False):
        lax.fori_loop(0, N_EXPERTS, functools.partial(per_segment, start=start), 0)


def _dispatch(counts, pad_start, padded, seg_src, seg_len, seg_dst, lpos, h2, cap):
    n, d = h2.shape
    tm = min(TOKEN_TILE, n)
    return pl.pallas_call(
        _dispatch_kernel,
        grid_spec=pltpu.PrefetchScalarGridSpec(
            num_scalar_prefetch=6,
            grid=(n // tm,),
            in_specs=[
                pl.BlockSpec((TOP_K, tm), lambda i, *_: (0, i)),
                pl.BlockSpec((tm, d), lambda i, *_: (i, 0)),
            ],
            out_specs=pl.BlockSpec(memory_space=pl.ANY),
            scratch_shapes=[pltpu.VMEM((_stage_rows(tm), d // 2), jnp.uint32),
                            pltpu.VMEM((MOE_BLOCK, d // 2), jnp.uint32),
                            pltpu.SemaphoreType.DMA, pltpu.SemaphoreType.DMA],
        ),
        out_shape=jax.ShapeDtypeStruct((cap, d // 2), jnp.uint32),
        compiler_params=pltpu.CompilerParams(
            dimension_semantics=("arbitrary",), vmem_limit_bytes=VMEM_LIMIT),
        name="dispatch",
    )(counts, pad_start, padded, seg_src, seg_len, seg_dst, lpos, h2)


def _expert_kernel(bexp_ref, nact_ref, xs_ref, w1_ref, b1_ref, w2_ref, b2_ref, y_ref, w1p_ref, w2b_ref):
    b = pl.program_id(0)
    active = b < nact_ref[0]
    fresh = jnp.logical_or(b == 0, bexp_ref[b] != bexp_ref[jnp.maximum(b - 1, 0)])
    gw = MXU_WIDTH
    half = gw // 2

    @pl.when(jnp.logical_not(active))
    def _():
        y_ref[...] = jnp.zeros_like(y_ref)

    @pl.when(jnp.logical_and(active, fresh))
    def _():
        r = lax.broadcasted_iota(jnp.int32, (gw, gw), 0)
        c = lax.broadcasted_iota(jnp.int32, (gw, gw), 1)
        perm = jnp.where(r == jnp.where(c < half, 2 * c, 2 * (c - half) + 1), 1.0, 0.0).astype(BF16)
        for j in range(w1_ref.shape[1] // gw):
            w = w1_ref[:, j * gw:(j + 1) * gw].astype(BF16)
            w1p_ref[:, j * gw:(j + 1) * gw] = jnp.dot(w, perm, preferred_element_type=F32).astype(BF16)
        w2b_ref[...] = w2_ref[...].astype(BF16)

    @pl.when(active)
    def _():
        x = _unpack_bf16_pairs(xs_ref[...])
        a = jnp.dot(x, w1p_ref[...], preferred_element_type=F32) + b1_ref[...]
        acts = []
        for j in range(a.shape[1] // gw):
            ag = jnp.minimum(a[:, j * gw:j * gw + half], SWIGLU_LIMIT)
            al = jnp.clip(a[:, j * gw + half:(j + 1) * gw], -SWIGLU_LIMIT, SWIGLU_LIMIT)
            acts.append((ag * (1.0 / (1.0 + jnp.exp(-SWIGLU_ALPHA * ag))) * (al + 1.0)).astype(BF16))
        act = jnp.concatenate(acts, axis=1)
        y = jnp.dot(act, w2b_ref[...], preferred_element_type=F32) + b2_ref[...]
        y_ref[...] = _pack_bf16_pairs(y)


def _experts(bexp, nact, xs, w1, b1p, w2, b2):
    cap, dh = xs.shape
    d = 2 * dh
    de2 = w1.shape[2]
    de = w2.shape[1]
    nblk = cap // MOE_BLOCK
    row = lambda b, be, na: (jnp.minimum(b, na[0] - 1), 0)
    exp3 = lambda b, be, na: (be[b], 0, 0)
    return pl.pallas_call(
        _expert_kernel,
        grid_spec=pltpu.PrefetchScalarGridSpec(
            num_scalar_prefetch=2,
            grid=(nblk,),
            in_specs=[
                pl.BlockSpec((MOE_BLOCK, dh), row),
                pl.BlockSpec((None, d, de2), exp3),
                pl.BlockSpec((None, 1, de2), exp3),
                pl.BlockSpec((None, de, d), exp3),
                pl.BlockSpec((None, 1, d), exp3),
            ],
            out_specs=pl.BlockSpec((MOE_BLOCK, dh), lambda b, be, na: (b, 0)),
            scratch_shapes=[pltpu.VMEM((d, de2), BF16), pltpu.VMEM((de, d), BF16)],
        ),
        out_shape=jax.ShapeDtypeStruct((cap, dh), jnp.uint32),
        compiler_params=pltpu.CompilerParams(
            dimension_semantics=("arbitrary",), vmem_limit_bytes=VMEM_LIMIT),
        name="experts",
    )(bexp, nact, xs, w1, b1p, w2, b2)


def _combine_kernel(ssrc_ref, slen_ref, sdst_ref, lpos_ref, gate_ref, y_ref, x1_ref, fg_ref, o_ref,
                    stage_ref, g_ref, sem):
    i = pl.program_id(0)
    tm = x1_ref.shape[0]

    def per_segment(e, carry, start):
        s = i * N_EXPERTS + e
        _segment_copies(y_ref, sdst_ref[s], stage_ref, ssrc_ref[s], slen_ref[s], tm, sem, start)
        return carry

    @pl.when(i == 0)
    def _():
        stage_ref[...] = jnp.zeros_like(stage_ref)

    lax.fori_loop(0, N_EXPERTS, functools.partial(per_segment, start=True), 0)

    chunk = MXU_WIDTH
    lp = [lpos_ref[:, k:k + 1] for k in range(TOP_K)]
    gt = [gate_ref[:, k:k + 1] for k in range(TOP_K)]
    for c in range(stage_ref.shape[0] // chunk):
        r = lax.broadcasted_iota(jnp.int32, (tm, chunk), 1) + c * chunk
        g = jnp.zeros((tm, chunk), F32)
        for k in range(TOP_K):
            g = jnp.where(r == lp[k], gt[k], g)
        g_ref[:, c * chunk:(c + 1) * chunk] = g.astype(BF16)

    lax.fori_loop(0, N_EXPERTS, functools.partial(per_segment, start=False), 0)
    y = _unpack_bf16_pairs(stage_ref[...])
    x = x1_ref[...] + jnp.dot(g_ref[...], y, preferred_element_type=F32)
    o_ref[...] = _rms(x, fg_ref[...])


def _combine(seg_src, seg_len, seg_dst, lpos_col, gate_col, y, x1, fg):
    n, d = x1.shape
    tm = min(TOKEN_TILE, n)
    return pl.pallas_call(
        _combine_kernel,
        grid_spec=pltpu.PrefetchScalarGridSpec(
            num_scalar_prefetch=3,
            grid=(n // tm,),
            in_specs=[
                pl.BlockSpec((tm, TOP_K), lambda i, *_: (i, 0)),
                pl.BlockSpec((tm, TOP_K), lambda i, *_: (i, 0)),
                pl.BlockSpec(memory_space=pl.ANY),
                pl.BlockSpec((tm, d), lambda i, *_: (i, 0)),
                pl.BlockSpec((1, d), lambda i, *_: (0, 0)),
            ],
            out_specs=pl.BlockSpec((tm, d), lambda i, *_: (i, 0)),
            scratch_shapes=[pltpu.VMEM((_stage_rows(tm), d // 2), jnp.uint32),
                            pltpu.VMEM((tm, _stage_rows(tm)), BF16),
                            pltpu.SemaphoreType.DMA],
        ),
        out_shape=jax.ShapeDtypeStruct((n, d), F32),
        compiler_params=pltpu.CompilerParams(
            dimension_semantics=("arbitrary",), vmem_limit_bytes=VMEM_LIMIT),
        name="combine",
    )(seg_src, seg_len, seg_dst, lpos_col, gate_col, y, x1, fg)


def _layer(x2, batch, seq, layer, lam_init, rel_bias_table, mix_norm_g, w_in, sgu_ln_g, sgu_ln_b, sgu_w,
           sgu_b, gmlp_out_g, lambda_q1, lambda_k1, lambda_q2, lambda_k2, diff_subln_g, w_out,
           ffn_norm_g, router_w, router_b, w_mlp1, b_mlp1, w_mlp2, b_mlp2):
    n, d = x2.shape
    l = layer
    row = lambda a: a[l].reshape(1, -1).astype(F32)

    gout, qt, k, vt = _mix_in(x2, row(mix_norm_g), w_in[l].astype(BF16), row(sgu_ln_g), row(sgu_ln_b),
                              sgu_w[l].astype(F32), sgu_b[l].astype(F32)[:, :, None], row(gmlp_out_g))
    dout = _diff_attn(qt, k, vt, _near_bias_tiles(rel_bias_table),
                      (row(lambda_q1), row(lambda_k1), row(lambda_q2), row(lambda_k2)),
                      row(diff_subln_g), batch, seq, lam_init)
    x1, h2, gate, lpos, lstart, lcnt, gbase, counts = _mix_out(
        x2, gout, dout, w_out[l].astype(BF16), row(ffn_norm_g),
        router_w[l].astype(F32).T, router_b[l].astype(F32)[:, None])

    counts = counts[:, 0]
    padded = (counts + MOE_BLOCK - 1) // MOE_BLOCK * MOE_BLOCK
    pad_end = jnp.cumsum(padded)
    pad_start = pad_end - padded
    n_tiles = lstart.shape[0] // N_EXPERTS
    nblk = pl.cdiv(n * TOP_K + n_tiles * N_EXPERTS * (SUBLANES - 1), MOE_BLOCK) + N_EXPERTS
    cap = nblk * MOE_BLOCK
    block_start = jnp.arange(nblk, dtype=jnp.int32) * MOE_BLOCK
    block_exp = jnp.minimum(jnp.sum(pad_end[None, :] <= block_start[:, None], axis=1),
                            N_EXPERTS - 1).astype(jnp.int32)
    n_active = (pad_end[-1] // MOE_BLOCK).astype(jnp.int32).reshape(1)
    seg_src = lstart[:, 0]
    seg_len = lcnt[:, 0]
    seg_dst = gbase[:, 0] + jnp.tile(pad_start, n_tiles)

    xs = _dispatch(counts, pad_start, padded, seg_src, seg_len, seg_dst, lpos, h2, cap)
    half = MXU_WIDTH // 2
    b1p = b_mlp1[l].astype(F32).reshape(N_EXPERTS, -1, half, 2).transpose(0, 1, 3, 2).reshape(N_EXPERTS, 1, -1)
    y = _experts(block_exp, n_active, xs, w_mlp1[l].astype(F32), b1p,
                 w_mlp2[l].astype(F32), b_mlp2[l][:, None, :].astype(F32))
    return (seg_src, seg_len, seg_dst, lpos.T, gate.T), y, x1


def kernel(x, rel_bias_table, mix_norm_g, w_in, sgu_ln_g, sgu_ln_b, sgu_w, sgu_b, gmlp_out_g, lambda_q1,
           lambda_k1, lambda_q2, lambda_k2, diff_subln_g, w_out, ffn_norm_g, router_w, router_b, w_mlp1,
           b_mlp1, w_mlp2, b_mlp2, final_norm_g):
    batch, seq, d = x.shape
    depth = w_in.shape[0]
    assert depth == 1, "the final RMSNorm is fused into the last layer's combine step"
    assert seq % TOKEN_TILE == 0 or batch * seq <= TOKEN_TILE
    x2 = x.reshape(batch * seq, d).astype(F32)
    lam_init = 0.8 - 0.6 * math.exp(-0.3 * 0)
    routing, y, x1 = _layer(
        x2, batch, seq, 0, lam_init, rel_bias_table, mix_norm_g, w_in, sgu_ln_g, sgu_ln_b, sgu_w, sgu_b,
        gmlp_out_g, lambda_q1, lambda_k1, lambda_q2, lambda_k2, diff_subln_g, w_out, ffn_norm_g,
        router_w, router_b, w_mlp1, b_mlp1, w_mlp2, b_mlp2)
    out = _combine(*routing, y, x1, final_norm_g.reshape(1, d).astype(F32))
    return out.reshape(batch, seq, d).astype(x.dtype)
```

```python
import functools
import math

import jax
import jax.numpy as jnp
from jax import lax
from jax.experimental import pallas as pl
from jax.experimental.pallas import tpu as pltpu

F32 = jnp.float32
BF16 = jnp.bfloat16

CHUNK = 64
GMLP_GROUPS = 4
SGU_BLOCK = 128
DIFF_HEADS = 4
DIFF_QK_DIM = 64
DIFF_V_DIM = 128
REL_BUCKETS = 32
REL_MAX_DIST = 128
N_EXPERTS = 32
TOP_K = 4
SWIGLU_ALPHA = 1.702
SWIGLU_LIMIT = 7.0
NORM_EPS = 1e-5
NEG_INF = -1e30

LANES = 128
SUBLANES = 8
MXU_WIDTH = 256
ATT_TILE = 512
LOG2E = 1.4426950408889634
TOKEN_TILE = 512
MOE_BLOCK = 512
VMEM_LIMIT = 56 * 1024 * 1024


def _rms(x, g):
    return x * lax.rsqrt(jnp.mean(x * x, axis=-1, keepdims=True) + NORM_EPS) * g


def _gelu(z):
    return 0.5 * z * (1.0 + lax.erf(z * (1.0 / math.sqrt(2.0))))


def _mix_in_kernel(x_ref, ng_ref, win_ref, lng_ref, lnb_ref, sw_ref, sb_ref, og_ref,
                   gout_ref, qt_ref, k_ref, vt_ref):
    tm = x_ref.shape[0]
    dg = gout_ref.shape[1]
    gw = dg // GMLP_GROUPS
    hb = _rms(x_ref[...], ng_ref[...]).astype(BF16)

    z = _gelu(jnp.dot(hb, win_ref[:, :2 * dg], preferred_element_type=F32))
    u = z[:, :dg]
    v = z[:, dg:]
    mu = jnp.mean(v, axis=-1, keepdims=True)
    vc = v - mu
    v = vc * lax.rsqrt(jnp.mean(vc * vc, axis=-1, keepdims=True) + NORM_EPS) * lng_ref[...] + lnb_ref[...]
    vb = v.astype(BF16)

    pos_i = lax.broadcasted_iota(jnp.int32, (SGU_BLOCK, SGU_BLOCK), 0) // CHUNK
    pos_j = lax.broadcasted_iota(jnp.int32, (SGU_BLOCK, SGU_BLOCK), 1) // CHUNK
    causal = pos_j <= pos_i
    rows = []
    for n in range(tm // SGU_BLOCK):
        cols = []
        for g in range(GMLP_GROUPS):
            w = jnp.where(causal, sw_ref[g], 0.0).astype(BF16)
            blk = vb[n * SGU_BLOCK:(n + 1) * SGU_BLOCK, g * gw:(g + 1) * gw]
            cols.append(jnp.dot(w, blk, preferred_element_type=F32) + sb_ref[g])
        rows.append(jnp.concatenate(cols, axis=1))
    sv = jnp.concatenate(rows, axis=0)
    gout_ref[...] = _rms(u * sv, og_ref[...]).astype(BF16)

    dd = DIFF_HEADS * LANES
    p = jnp.dot(hb, win_ref[:, 2 * dg:], preferred_element_type=F32)
    scale = DIFF_QK_DIM ** -0.5 * LOG2E
    for h in range(DIFF_HEADS):
        q = p[:, h * LANES:(h + 1) * LANES] * scale
        k = p[:, dd + h * LANES:dd + (h + 1) * LANES]
        vv = p[:, 2 * dd + h * LANES:2 * dd + (h + 1) * LANES]
        k_ref[h] = k.astype(BF16)
        qt = q.T.astype(BF16)
        vt = vv.T.astype(BF16)
        for t in range(tm // ATT_TILE):
            qt_ref[h, t] = qt[:, t * ATT_TILE:(t + 1) * ATT_TILE]
            vt_ref[h, t] = vt[:, t * ATT_TILE:(t + 1) * ATT_TILE]


def _mix_in(x2, ng, win, lng, lnb, sw, sb, og):
    n, d = x2.shape
    dg = lng.shape[1]
    tm = min(TOKEN_TILE, n)
    nt = tm // ATT_TILE
    const = lambda *shape: pl.BlockSpec(shape, lambda i: (0,) * len(shape))
    return pl.pallas_call(
        _mix_in_kernel,
        grid=(n // tm,),
        in_specs=[
            pl.BlockSpec((tm, d), lambda i: (i, 0)),
            const(1, d),
            const(*win.shape),
            const(1, dg),
            const(1, dg),
            const(*sw.shape),
            const(*sb.shape),
            const(1, dg),
        ],
        out_specs=[
            pl.BlockSpec((tm, dg), lambda i: (i, 0)),
            pl.BlockSpec((DIFF_HEADS, nt, LANES, ATT_TILE), lambda i: (0, i, 0, 0)),
            pl.BlockSpec((DIFF_HEADS, tm, LANES), lambda i: (0, i, 0)),
            pl.BlockSpec((DIFF_HEADS, nt, LANES, ATT_TILE), lambda i: (0, i, 0, 0)),
        ],
        out_shape=[
            jax.ShapeDtypeStruct((n, dg), BF16),
            jax.ShapeDtypeStruct((DIFF_HEADS, n // ATT_TILE, LANES, ATT_TILE), BF16),
            jax.ShapeDtypeStruct((DIFF_HEADS, n, LANES), BF16),
            jax.ShapeDtypeStruct((DIFF_HEADS, n // ATT_TILE, LANES, ATT_TILE), BF16),
        ],
        compiler_params=pltpu.CompilerParams(
            dimension_semantics=("arbitrary",), vmem_limit_bytes=VMEM_LIMIT),
        name="mix_in",
    )(x2, ng, win, lng, lnb, sw, sb, og)


def _attn_kernel(lq1_ref, lk1_ref, lq2_ref, lk2_ref, qt_ref, k_ref, vt_ref, bias_ref, g_ref,
                 o_ref, acc_ref, m_ref, l_ref, qwa_ref, qwb_ref, s0_ref, s1_ref, cm0_ref, cm1_ref, *, lam_init):
    pi = pl.program_id(2)
    t = ATT_TILE
    qa = 2 * pi
    qb = qa + 1
    b0, b1 = (s0_ref, cm0_ref), (s1_ref, cm1_ref)

    for n, qw_ref in enumerate((qwa_ref, qwb_ref)):
        qt = qt_ref[n]
        drow = lax.broadcasted_iota(jnp.int32, qt.shape, 0)
        zero = jnp.zeros_like(qt)
        qw_ref[:, :t] = jnp.where(drow < DIFF_QK_DIM, qt, zero)
        qw_ref[:, t:] = jnp.where(drow >= DIFF_QK_DIM, qt, zero)

    def produce(j, buf, kind, qw_ref):
        s_ref, cm_ref = buf
        k = k_ref[pl.ds(pl.multiple_of(j * t, t), t), :]
        s = jnp.dot(k, qw_ref[...], preferred_element_type=F32)
        if kind != "far":
            bias = bias_ref[0] if kind == "diag" else bias_ref[1]
            halves = [s[:, :t] + bias, s[:, t:] + bias]
            if kind == "diag":
                kc = lax.broadcasted_iota(jnp.int32, (t, t), 0) // CHUNK
                qc = lax.broadcasted_iota(jnp.int32, (t, t), 1) // CHUNK
                halves = [jnp.where(kc <= qc, h, NEG_INF) for h in halves]
            s = jnp.concatenate(halves, axis=1)
        s_ref[...] = s
        cm_ref[...] = jnp.max(s, axis=0, keepdims=True)

    def consume(j, buf):
        s_ref, cm_ref = buf
        m_old = m_ref[...]
        m_new = jnp.maximum(m_old, cm_ref[...])
        alpha = jnp.exp2(m_old - m_new)
        p = jnp.exp2(s_ref[...] - m_new)
        l_ref[...] = alpha * l_ref[...] + jnp.sum(p, axis=0, keepdims=True)
        m_ref[...] = m_new
        acc_ref[...] = alpha * acc_ref[...] + jnp.dot(vt_ref[j], p.astype(BF16), preferred_element_type=F32)

    def step(j, cbuf, pbuf, kind, qw_ref):
        produce(j + 1, pbuf, kind, qw_ref)
        consume(j, cbuf)

    def far_pairs(first, count, qw_ref):
        def body(i, carry):
            j = first + 2 * i
            step(j, b0, b1, "far", qw_ref)
            step(j + 1, b1, b0, "far", qw_ref)
            return carry
        lax.fori_loop(0, count, body, 0)

    def init():
        acc_ref[...] = jnp.zeros_like(acc_ref)
        m_ref[...] = jnp.full_like(m_ref, NEG_INF)
        l_ref[...] = jnp.zeros_like(l_ref)

    def finalize(row):
        lam = (jnp.exp(jnp.sum(lq1_ref[...] * lk1_ref[...], keepdims=True))
               - jnp.exp(jnp.sum(lq2_ref[...] * lk2_ref[...], keepdims=True)) + lam_init)
        on = acc_ref[...] * (1.0 / l_ref[...])
        o = on[:, :t] - lam * on[:, t:]
        o = o * lax.rsqrt(jnp.mean(o * o, axis=0, keepdims=True) + NORM_EPS)
        o_ref[row:row + t, :] = (o.T * (g_ref[...] * (1.0 - lam_init))).astype(BF16)

    init()

    @pl.when(pi == 0)
    def _():
        produce(0, b0, "diag", qwa_ref)

    @pl.when(pi >= 1)
    def _():
        produce(0, b0, "far", qwa_ref)

    far_pairs(0, jnp.maximum(pi - 1, 0), qwa_ref)

    @pl.when(pi >= 1)
    def _():
        step(qa - 2, b0, b1, "sub", qwa_ref)
        step(qa - 1, b1, b0, "diag", qwa_ref)

    @pl.when(pi == 0)
    def _():
        produce(0, b1, "sub", qwb_ref)
        consume(qa, b0)

    @pl.when(pi >= 1)
    def _():
        produce(0, b1, "far", qwb_ref)
        consume(qa, b0)

    finalize(0)
    init()

    @pl.when(pi >= 1)
    def _():
        step(0, b1, b0, "far", qwb_ref)

    far_pairs(1, jnp.maximum(pi - 1, 0), qwb_ref)

    @pl.when(pi >= 1)
    def _():
        step(qb - 2, b0, b1, "sub", qwb_ref)

    step(qb - 1, b1, b0, "diag", qwb_ref)
    consume(qb, b0)
    finalize(t)


def _diff_attn(qt, k, vt, bias, lams, subln_g, batch, seq, lam_init):
    t = ATT_TILE
    nq = seq // t
    assert nq % 2 == 0
    npair = nq // 2
    lam_spec = pl.BlockSpec((1, DIFF_QK_DIM), lambda b, h, i: (0, 0))
    return pl.pallas_call(
        functools.partial(_attn_kernel, lam_init=lam_init),
        grid=(batch, DIFF_HEADS, npair),
        in_specs=[
            lam_spec, lam_spec, lam_spec, lam_spec,
            pl.BlockSpec((None, 2, LANES, t), lambda b, h, i: (h, b * npair + i, 0, 0)),
            pl.BlockSpec((None, seq, LANES), lambda b, h, i: (h, b, 0)),
            pl.BlockSpec((None, nq, LANES, t), lambda b, h, i: (h, b, 0, 0)),
            pl.BlockSpec((None, 2, t, t), lambda b, h, i: (h, 0, 0, 0)),
            pl.BlockSpec((1, DIFF_V_DIM), lambda b, h, i: (0, 0)),
        ],
        out_specs=pl.BlockSpec((2 * t, DIFF_V_DIM), lambda b, h, i: (b * npair + i, h)),
        out_shape=jax.ShapeDtypeStruct((batch * seq, DIFF_HEADS * DIFF_V_DIM), BF16),
        scratch_shapes=[
            pltpu.VMEM((DIFF_V_DIM, 2 * t), F32),
            pltpu.VMEM((1, 2 * t), F32),
            pltpu.VMEM((1, 2 * t), F32),
            pltpu.VMEM((LANES, 2 * t), BF16),
            pltpu.VMEM((LANES, 2 * t), BF16),
            pltpu.VMEM((t, 2 * t), F32),
            pltpu.VMEM((t, 2 * t), F32),
            pltpu.VMEM((1, 2 * t), F32),
            pltpu.VMEM((1, 2 * t), F32),
        ],
        compiler_params=pltpu.CompilerParams(
            dimension_semantics=("arbitrary", "arbitrary", "arbitrary"), vmem_limit_bytes=VMEM_LIMIT),
        name="diff_attn",
    )(*lams, qt, k, vt, bias, subln_g)


def _rel_bucket(rel):
    nb = REL_BUCKETS // 2
    max_exact = nb // 2
    side = jnp.where(rel > 0, nb, 0)
    n = jnp.abs(rel)
    nf = jnp.maximum(n, 1).astype(F32)
    large = max_exact + (jnp.log(nf / max_exact) / math.log(REL_MAX_DIST / max_exact)
                         * (nb - max_exact)).astype(jnp.int32)
    large = jnp.minimum(large, nb - 1)
    return side + jnp.where(n < max_exact, n, large)


def _near_bias_tiles(rel_table):
    t = ATT_TILE
    kpos = jnp.arange(t)[:, None]
    qpos = jnp.arange(t)[None, :]
    rel = jnp.stack([kpos - qpos, kpos - qpos - t])
    onehot = _rel_bucket(rel)[..., None] == jnp.arange(REL_BUCKETS)
    table = rel_table.astype(F32)
    far = table[REL_BUCKETS // 2 - 1]
    tiles = [jnp.sum(jnp.where(onehot, table[:, h], 0.0), axis=-1) - far[h] for h in range(DIFF_HEADS)]
    return jnp.stack(tiles) * LOG2E


def _mix_out_kernel(x_ref, g_ref, d_ref, wo_ref, ng_ref, rwt_ref, rb_ref,
                    x1_ref, h2_ref, gate_ref, lpos_ref, lstart_ref, lcnt_ref, gbase_ref, cnt_ref, base_ref):
    i = pl.program_id(0)
    tm = x_ref.shape[0]
    dg = g_ref.shape[1]

    @pl.when(i == 0)
    def _():
        base_ref[...] = jnp.zeros_like(base_ref)

    mix = (jnp.dot(g_ref[...], wo_ref[:dg, :], preferred_element_type=F32)
           + jnp.dot(d_ref[...], wo_ref[dg:, :], preferred_element_type=F32))
    x1 = x_ref[...] + mix
    x1_ref[...] = x1
    h2 = _rms(x1, ng_ref[...])
    h_hi = h2.astype(BF16)
    h2_ref[...] = h_hi

    h_lo = (h2 - h_hi.astype(F32)).astype(BF16)
    rw = rwt_ref[...]
    r_hi = rw.astype(BF16)
    r_lo = (rw - r_hi.astype(F32)).astype(BF16)
    nt_dot = lambda a, b: lax.dot_general(a, b, (((1,), (1,)), ((), ())), preferred_element_type=F32)
    logits = nt_dot(r_hi, h_hi) + (nt_dot(r_hi, h_lo) + nt_dot(r_lo, h_hi)) + rb_ref[...]
    eidx = lax.broadcasted_iota(jnp.int32, logits.shape, 0)
    cur = logits
    vals, sels = [], []
    for k in range(TOP_K):
        mx = jnp.max(cur, axis=0, keepdims=True)
        idx = jnp.min(jnp.where(cur == mx, eidx, N_EXPERTS), axis=0, keepdims=True)
        sel = eidx == idx
        vals.append(mx)
        sels.append(sel)
        cur = jnp.where(sel, -jnp.inf, cur)
    ex = [jnp.exp(v - vals[0]) for v in vals]
    inv = 1.0 / (ex[0] + ex[1] + ex[2] + ex[3])
    for k in range(TOP_K):
        gate_ref[k:k + 1, :] = ex[k] * inv

    member = sels[0] | sels[1] | sels[2] | sels[3]
    mem_b = jnp.where(member, 1.0, 0.0).astype(BF16)
    ri = lax.broadcasted_iota(jnp.int32, (tm, tm), 0)
    ci = lax.broadcasted_iota(jnp.int32, (tm, tm), 1)
    tri = jnp.where(ri < ci, 1.0, 0.0).astype(BF16)
    excl = jnp.dot(mem_b, tri, preferred_element_type=F32)
    er = lax.broadcasted_iota(jnp.int32, (N_EXPERTS, N_EXPERTS), 0)
    ec = lax.broadcasted_iota(jnp.int32, (N_EXPERTS, N_EXPERTS), 1)
    lower = jnp.where(ec < er, 1.0, 0.0).astype(BF16)
    lcnt = jnp.floor((jnp.sum(mem_b.astype(F32), axis=1, keepdims=True) + (SUBLANES - 1)) * (1.0 / SUBLANES))
    groups = jnp.broadcast_to(lcnt, (N_EXPERTS, LANES)).astype(BF16)
    lstart = jnp.dot(lower, groups, preferred_element_type=F32)[:, :1] * SUBLANES
    lcnt = lcnt * SUBLANES
    lrank = lstart + excl
    for k in range(TOP_K):
        lpos_ref[k:k + 1, :] = jnp.sum(jnp.where(sels[k], lrank, 0.0), axis=0, keepdims=True).astype(jnp.int32)
    gbase = base_ref[...]
    lstart_ref[...] = jnp.broadcast_to(lstart, lstart_ref.shape).astype(jnp.int32)
    lcnt_ref[...] = jnp.broadcast_to(lcnt, lcnt_ref.shape).astype(jnp.int32)
    gbase_ref[...] = jnp.broadcast_to(gbase, gbase_ref.shape).astype(jnp.int32)
    base_ref[...] = gbase + lcnt
    cnt_ref[...] = jnp.broadcast_to(gbase + lcnt, cnt_ref.shape).astype(jnp.int32)


def _mix_out(x2, gout, dout, wo, ng, rwt, rb):
    n, d = x2.shape
    dg = gout.shape[1]
    tm = min(TOKEN_TILE, n)
    nt = n // tm
    const = lambda *shape: pl.BlockSpec(shape, lambda i: (0,) * len(shape))
    tok = lambda w: pl.BlockSpec((tm, w), lambda i: (i, 0))
    slot = pl.BlockSpec((TOP_K, tm), lambda i: (0, i))
    seg = pl.BlockSpec((N_EXPERTS, LANES), lambda i: (i, 0))
    seg_shape = jax.ShapeDtypeStruct((nt * N_EXPERTS, LANES), jnp.int32)
    return pl.pallas_call(
        _mix_out_kernel,
        grid=(nt,),
        in_specs=[tok(d), tok(dg), tok(dout.shape[1]), const(*wo.shape), const(1, d),
                  const(*rwt.shape), const(N_EXPERTS, 1)],
        out_specs=[tok(d), tok(d), slot, slot, seg, seg, seg, const(N_EXPERTS, LANES)],
        out_shape=[
            jax.ShapeDtypeStruct((n, d), F32),
            jax.ShapeDtypeStruct((n, d), BF16),
            jax.ShapeDtypeStruct((TOP_K, n), F32),
            jax.ShapeDtypeStruct((TOP_K, n), jnp.int32),
            seg_shape, seg_shape, seg_shape,
            jax.ShapeDtypeStruct((N_EXPERTS, LANES), jnp.int32),
        ],
        scratch_shapes=[pltpu.VMEM((N_EXPERTS, 1), F32)],
        compiler_params=pltpu.CompilerParams(
            dimension_semantics=("arbitrary",), vmem_limit_bytes=VMEM_LIMIT),
        name="mix_out",
    )(x2, gout, dout, wo, ng, rwt, rb)


def _stage_rows(tm):
    return pl.cdiv(TOP_K * tm + N_EXPERTS * (SUBLANES - 1), MXU_WIDTH) * MXU_WIDTH


def _segment_copies(src_ref, src_row, dst_ref, dst_row, length, max_len, sem, start):
    size = SUBLANES
    while size <= max_len:
        off = length & (size - 1)

        @pl.when((length & size) != 0)
        def _(size=size, off=off):
            cp = pltpu.make_async_copy(src_ref.at[pl.ds(pl.multiple_of(src_row + off, SUBLANES), size), :],
                                       dst_ref.at[pl.ds(pl.multiple_of(dst_row + off, SUBLANES), size), :], sem)
            cp.start() if start else cp.wait()
        size *= 2


def _pack_bf16_pairs(x):
    c = x.shape[1] // 2
    xb = x.astype(BF16).astype(F32)
    lo = lax.shift_right_logical(lax.bitcast_convert_type(xb[:, :c], jnp.uint32), jnp.uint32(16))
    hi = lax.bitcast_convert_type(xb[:, c:], jnp.uint32) & jnp.uint32(0xFFFF0000)
    return hi | lo


def _unpack_bf16_pairs(w):
    lo = lax.bitcast_convert_type(lax.shift_left(w, jnp.uint32(16)), F32)
    hi = lax.bitcast_convert_type(w & jnp.uint32(0xFFFF0000), F32)
    return jnp.concatenate([lo, hi], axis=1).astype(BF16)


def _dispatch_kernel(cnt_ref, pstart_ref, padded_ref, ssrc_ref, slen_ref, sdst_ref,
                     lpos_ref, h_ref, xs_ref, stage_ref, zero_ref, sem, zsem):
    i = pl.program_id(0)
    tm = h_ref.shape[0]
    zr = zero_ref.shape[0]

    @pl.when(i == 0)
    def _():
        zero_ref[...] = jnp.zeros_like(zero_ref)
        used = pstart_ref[N_EXPERTS - 1] + padded_ref[N_EXPERTS - 1]

        def pad_rows(start):
            def per_expert(e, carry):
                _segment_copies(zero_ref, 0, xs_ref, pstart_ref[e] + cnt_ref[e], padded_ref[e] - cnt_ref[e],
                                zr, zsem, start)
                return carry
            lax.fori_loop(0, N_EXPERTS, per_expert, 0)

            def per_tail(r, carry):
                cp = pltpu.make_async_copy(
                    zero_ref, xs_ref.at[pl.ds(pl.multiple_of(used + r * zr, zr), zr), :], zsem)
                cp.start() if start else cp.wait()
                return carry
            lax.fori_loop(0, (xs_ref.shape[0] - used) // zr, per_tail, 0)

        pad_rows(True)
        pad_rows(False)

    slot = i & 1
    stage = stage_ref.at[slot]
    h = h_ref[...]
    chunk = MXU_WIDTH
    lp = [lpos_ref[k:k + 1, :] for k in range(TOP_K)]
    for c in range(stage.shape[0] // chunk):
        r = lax.broadcasted_iota(jnp.int32, (chunk, tm), 0) + c * chunk
        hit = (r == lp[0]) | (r == lp[1]) | (r == lp[2]) | (r == lp[3])
        perm = jnp.where(hit, 1.0, 0.0).astype(BF16)
        rows = jnp.dot(perm, h, preferred_element_type=F32)
        stage[c * chunk:(c + 1) * chunk, :] = _pack_bf16_pairs(rows)

    def segments(tile, tile_slot, start):
        def per_segment(e, carry):
            s = tile * N_EXPERTS + e
            _segment_copies(stage_ref.at[tile_slot], ssrc_ref[s], xs_ref, sdst_ref[s], slen_ref[s], tm,
                            sem.at[tile_slot], start)
            return carry
        lax.fori_loop(0, N_EXPERTS, per_segment, 0)

    segments(i, slot, True)

    @pl.when(i > 0)
    def _():
        segments(i - 1, 1 - slot, False)

    @pl.when(i == pl.num_programs(0) - 1)
    def _():
        segments(i, slot, False)


def _dispatch(counts, pad_start, padded, seg_src, seg_len, seg_dst, lpos, h2, cap):
    n, d = h2.shape
    tm = min(TOKEN_TILE, n)
    return pl.pallas_call(
        _dispatch_kernel,
        grid_spec=pltpu.PrefetchScalarGridSpec(
            num_scalar_prefetch=6,
            grid=(n // tm,),
            in_specs=[
                pl.BlockSpec((TOP_K, tm), lambda i, *_: (0, i)),
                pl.BlockSpec((tm, d), lambda i, *_: (i, 0)),
            ],
            out_specs=pl.BlockSpec(memory_space=pl.ANY),
            scratch_shapes=[pltpu.VMEM((2, _stage_rows(tm), d // 2), jnp.uint32),
                            pltpu.VMEM((MOE_BLOCK, d // 2), jnp.uint32),
                            pltpu.SemaphoreType.DMA((2,)), pltpu.SemaphoreType.DMA],
        ),
        out_shape=jax.ShapeDtypeStruct((cap, d // 2), jnp.uint32),
        compiler_params=pltpu.CompilerParams(
            dimension_semantics=("arbitrary",), vmem_limit_bytes=VMEM_LIMIT),
        name="dispatch",
    )(counts, pad_start, padded, seg_src, seg_len, seg_dst, lpos, h2)


def _expert_kernel(bexp_ref, nact_ref, xs_ref, w1_ref, b1_ref, w2_ref, b2_ref, y_ref, w1p_ref, w2b_ref):
    b = pl.program_id(0)
    active = b < nact_ref[0]
    fresh = jnp.logical_or(b == 0, bexp_ref[b] != bexp_ref[jnp.maximum(b - 1, 0)])
    gw = MXU_WIDTH
    half = gw // 2

    @pl.when(jnp.logical_not(active))
    def _():
        y_ref[...] = jnp.zeros_like(y_ref)

    @pl.when(jnp.logical_and(active, fresh))
    def _():
        r = lax.broadcasted_iota(jnp.int32, (gw, gw), 0)
        c = lax.broadcasted_iota(jnp.int32, (gw, gw), 1)
        perm = jnp.where(r == jnp.where(c < half, 2 * c, 2 * (c - half) + 1), 1.0, 0.0).astype(BF16)
        for j in range(w1_ref.shape[1] // gw):
            w = w1_ref[:, j * gw:(j + 1) * gw].astype(BF16)
            w1p_ref[:, j * gw:(j + 1) * gw] = jnp.dot(w, perm, preferred_element_type=F32).astype(BF16)
        w2b_ref[...] = w2_ref[...].astype(BF16)

    @pl.when(active)
    def _():
        x = _unpack_bf16_pairs(xs_ref[...])
        a = jnp.dot(x, w1p_ref[...], preferred_element_type=F32) + b1_ref[...]
        acts = []
        for j in range(a.shape[1] // gw):
            ag = jnp.minimum(a[:, j * gw:j * gw + half], SWIGLU_LIMIT)
            al = jnp.clip(a[:, j * gw + half:(j + 1) * gw], -SWIGLU_LIMIT, SWIGLU_LIMIT)
            acts.append((ag * (1.0 / (1.0 + jnp.exp(-SWIGLU_ALPHA * ag))) * (al + 1.0)).astype(BF16))
        act = jnp.concatenate(acts, axis=1)
        y = jnp.dot(act, w2b_ref[...], preferred_element_type=F32) + b2_ref[...]
        y_ref[...] = _pack_bf16_pairs(y)


def _experts(bexp, nact, xs, w1, b1p, w2, b2):
    cap, dh = xs.shape
    d = 2 * dh
    de2 = w1.shape[2]
    de = w2.shape[1]
    nblk = cap // MOE_BLOCK
    row = lambda b, be, na: (jnp.minimum(b, na[0] - 1), 0)
    exp3 = lambda b, be, na: (be[b], 0, 0)
    return pl.pallas_call(
        _expert_kernel,
        grid_spec=pltpu.PrefetchScalarGridSpec(
            num_scalar_prefetch=2,
            grid=(nblk,),
            in_specs=[
                pl.BlockSpec((MOE_BLOCK, dh), row),
                pl.BlockSpec((None, d, de2), exp3),
                pl.BlockSpec((None, 1, de2), exp3),
                pl.BlockSpec((None, de, d), exp3),
                pl.BlockSpec((None, 1, d), exp3),
            ],
            out_specs=pl.BlockSpec((MOE_BLOCK, dh), lambda b, be, na: (b, 0)),
            scratch_shapes=[pltpu.VMEM((d, de2), BF16), pltpu.VMEM((de, d), BF16)],
        ),
        out_shape=jax.ShapeDtypeStruct((cap, dh), jnp.uint32),
        compiler_params=pltpu.CompilerParams(
            dimension_semantics=("arbitrary",), vmem_limit_bytes=VMEM_LIMIT),
        name="experts",
    )(bexp, nact, xs, w1, b1p, w2, b2)


def _combine_kernel(ssrc_ref, slen_ref, sdst_ref, lpos_ref, gate_ref, y_ref, x1_ref, fg_ref, o_ref,
                    stage_ref, g_ref, sem):
    i = pl.program_id(0)
    tm = x1_ref.shape[0]

    slot = i & 1

    def segments(tile, tile_slot, start):
        def per_segment(e, carry):
            s = tile * N_EXPERTS + e
            _segment_copies(y_ref, sdst_ref[s], stage_ref.at[tile_slot], ssrc_ref[s], slen_ref[s], tm,
                            sem.at[tile_slot], start)
            return carry
        lax.fori_loop(0, N_EXPERTS, per_segment, 0)

    @pl.when(i == 0)
    def _():
        stage_ref[...] = jnp.zeros_like(stage_ref)
        segments(0, 0, True)

    @pl.when(i + 1 < pl.num_programs(0))
    def _():
        segments(i + 1, 1 - slot, True)

    chunk = MXU_WIDTH
    lp = [lpos_ref[:, k:k + 1] for k in range(TOP_K)]
    gt = [gate_ref[:, k:k + 1] for k in range(TOP_K)]
    for c in range(stage_ref.shape[1] // chunk):
        r = lax.broadcasted_iota(jnp.int32, (tm, chunk), 1) + c * chunk
        g = jnp.zeros((tm, chunk), F32)
        for k in range(TOP_K):
            g = jnp.where(r == lp[k], gt[k], g)
        g_ref[:, c * chunk:(c + 1) * chunk] = g.astype(BF16)

    segments(i, slot, False)
    y = _unpack_bf16_pairs(stage_ref[slot])
    x = x1_ref[...] + jnp.dot(g_ref[...], y, preferred_element_type=F32)
    o_ref[...] = _rms(x, fg_ref[...])


def _combine(seg_src, seg_len, seg_dst, lpos_col, gate_col, y, x1, fg):
    n, d = x1.shape
    tm = min(TOKEN_TILE, n)
    return pl.pallas_call(
        _combine_kernel,
        grid_spec=pltpu.PrefetchScalarGridSpec(
            num_scalar_prefetch=3,
            grid=(n // tm,),
            in_specs=[
                pl.BlockSpec((tm, TOP_K), lambda i, *_: (i, 0)),
                pl.BlockSpec((tm, TOP_K), lambda i, *_: (i, 0)),
                pl.BlockSpec(memory_space=pl.ANY),
                pl.BlockSpec((tm, d), lambda i, *_: (i, 0)),
                pl.BlockSpec((1, d), lambda i, *_: (0, 0)),
            ],
            out_specs=pl.BlockSpec((tm, d), lambda i, *_: (i, 0)),
            scratch_shapes=[pltpu.VMEM((2, _stage_rows(tm), d // 2), jnp.uint32),
                            pltpu.VMEM((tm, _stage_rows(tm)), BF16),
                            pltpu.SemaphoreType.DMA((2,))],
        ),
        out_shape=jax.ShapeDtypeStruct((n, d), F32),
        compiler_params=pltpu.CompilerParams(
            dimension_semantics=("arbitrary",), vmem_limit_bytes=VMEM_LIMIT),
        name="combine",
    )(seg_src, seg_len, seg_dst, lpos_col, gate_col, y, x1, fg)


def _layer(x2, batch, seq, layer, lam_init, rel_bias_table, mix_norm_g, w_in, sgu_ln_g, sgu_ln_b, sgu_w,
           sgu_b, gmlp_out_g, lambda_q1, lambda_k1, lambda_q2, lambda_k2, diff_subln_g, w_out,
           ffn_norm_g, router_w, router_b, w_mlp1, b_mlp1, w_mlp2, b_mlp2):
    n, d = x2.shape
    l = layer
    row = lambda a: a[l].reshape(1, -1).astype(F32)

    gout, qt, k, vt = _mix_in(x2, row(mix_norm_g), w_in[l].astype(BF16), row(sgu_ln_g), row(sgu_ln_b),
                              sgu_w[l].astype(F32), sgu_b[l].astype(F32)[:, :, None], row(gmlp_out_g))
    dout = _diff_attn(qt, k, vt, _near_bias_tiles(rel_bias_table),
                      (row(lambda_q1), row(lambda_k1), row(lambda_q2), row(lambda_k2)),
                      row(diff_subln_g), batch, seq, lam_init)
    x1, h2, gate, lpos, lstart, lcnt, gbase, counts = _mix_out(
        x2, gout, dout, w_out[l].astype(BF16), row(ffn_norm_g),
        router_w[l].astype(F32).T, router_b[l].astype(F32)[:, None])

    counts = counts[:, 0]
    padded = (counts + MOE_BLOCK - 1) // MOE_BLOCK * MOE_BLOCK
    pad_end = jnp.cumsum(padded)
    pad_start = pad_end - padded
    n_tiles = lstart.shape[0] // N_EXPERTS
    nblk = pl.cdiv(n * TOP_K + n_tiles * N_EXPERTS * (SUBLANES - 1), MOE_BLOCK) + N_EXPERTS
    cap = nblk * MOE_BLOCK
    block_start = jnp.arange(nblk, dtype=jnp.int32) * MOE_BLOCK
    block_exp = jnp.minimum(jnp.sum(pad_end[None, :] <= block_start[:, None], axis=1),
                            N_EXPERTS - 1).astype(jnp.int32)
    n_active = (pad_end[-1] // MOE_BLOCK).astype(jnp.int32).reshape(1)
    seg_src = lstart[:, 0]
    seg_len = lcnt[:, 0]
    seg_dst = gbase[:, 0] + jnp.tile(pad_start, n_tiles)

    xs = _dispatch(counts, pad_start, padded, seg_src, seg_len, seg_dst, lpos, h2, cap)
    half = MXU_WIDTH // 2
    b1p = b_mlp1[l].astype(F32).reshape(N_EXPERTS, -1, half, 2).transpose(0, 1, 3, 2).reshape(N_EXPERTS, 1, -1)
    y = _experts(block_exp, n_active, xs, w_mlp1[l].astype(F32), b1p,
                 w_mlp2[l].astype(F32), b_mlp2[l][:, None, :].astype(F32))
    return (seg_src, seg_len, seg_dst, lpos.T, gate.T), y, x1


def kernel(x, rel_bias_table, mix_norm_g, w_in, sgu_ln_g, sgu_ln_b, sgu_w, sgu_b, gmlp_out_g, lambda_q1,
           lambda_k1, lambda_q2, lambda_k2, diff_subln_g, w_out, ffn_norm_g, router_w, router_b, w_mlp1,
           b_mlp1, w_mlp2, b_mlp2, final_norm_g):
    batch, seq, d = x.shape
    depth = w_in.shape[0]
    assert depth == 1, "the final RMSNorm is fused into the last layer's combine step"
    assert seq % (2 * ATT_TILE) == 0 and ATT_TILE == TOKEN_TILE
    x2 = x.reshape(batch * seq, d).astype(F32)
    lam_init = 0.8 - 0.6 * math.exp(-0.3 * 0)
    routing, y, x1 = _layer(
        x2, batch, seq, 0, lam_init, rel_bias_table, mix_norm_g, w_in, sgu_ln_g, sgu_ln_b, sgu_w, sgu_b,
        gmlp_out_g, lambda_q1, lambda_k1, lambda_q2, lambda_k2, diff_subln_g, w_out, ffn_norm_g,
        router_w, router_b, w_mlp1, b_mlp1, w_mlp2, b_mlp2)
    out = _combine(*routing, y, x1, final_norm_g.reshape(1, d).astype(F32))
    return out.reshape(batch, seq, d).astype(x.dtype)
```

```python
import functools
import math

import jax
import jax.numpy as jnp
from jax import lax
from jax.experimental import pallas as pl
from jax.experimental.pallas import tpu as pltpu

F32 = jnp.float32
BF16 = jnp.bfloat16

CHUNK = 64
GMLP_GROUPS = 4
SGU_BLOCK = 128
DIFF_HEADS = 4
DIFF_QK_DIM = 64
DIFF_V_DIM = 128
REL_BUCKETS = 32
REL_MAX_DIST = 128
N_EXPERTS = 32
TOP_K = 4
SWIGLU_ALPHA = 1.702
SWIGLU_LIMIT = 7.0
NORM_EPS = 1e-5
NEG_INF = -1e30

LANES = 128
SUBLANES = 8
MXU_WIDTH = 256
ATT_TILE = 512
LOG2E = 1.4426950408889634
TOKEN_TILE = 512
MOE_BLOCK = 512
VMEM_LIMIT = 56 * 1024 * 1024


def _rms(x, g):
    return x * lax.rsqrt(jnp.mean(x * x, axis=-1, keepdims=True) + NORM_EPS) * g


def _gelu(z):
    return 0.5 * z * (1.0 + lax.erf(z * (1.0 / math.sqrt(2.0))))


def _mix_in_kernel(x_ref, ng_ref, win_ref, lng_ref, lnb_ref, sw_ref, sb_ref, og_ref,
                   gout_ref, qt_ref, k_ref, vt_ref):
    tm = x_ref.shape[0]
    dg = gout_ref.shape[1]
    gw = dg // GMLP_GROUPS
    hb = _rms(x_ref[...], ng_ref[...]).astype(BF16)

    z = _gelu(jnp.dot(hb, win_ref[:, :2 * dg], preferred_element_type=F32))
    u = z[:, :dg]
    v = z[:, dg:]
    mu = jnp.mean(v, axis=-1, keepdims=True)
    vc = v - mu
    v = vc * lax.rsqrt(jnp.mean(vc * vc, axis=-1, keepdims=True) + NORM_EPS) * lng_ref[...] + lnb_ref[...]
    vb = v.astype(BF16)

    pos_i = lax.broadcasted_iota(jnp.int32, (SGU_BLOCK, SGU_BLOCK), 0) // CHUNK
    pos_j = lax.broadcasted_iota(jnp.int32, (SGU_BLOCK, SGU_BLOCK), 1) // CHUNK
    causal = pos_j <= pos_i
    rows = []
    for n in range(tm // SGU_BLOCK):
        cols = []
        for g in range(GMLP_GROUPS):
            w = jnp.where(causal, sw_ref[g], 0.0).astype(BF16)
            blk = vb[n * SGU_BLOCK:(n + 1) * SGU_BLOCK, g * gw:(g + 1) * gw]
            cols.append(jnp.dot(w, blk, preferred_element_type=F32) + sb_ref[g])
        rows.append(jnp.concatenate(cols, axis=1))
    sv = jnp.concatenate(rows, axis=0)
    gout_ref[...] = _rms(u * sv, og_ref[...]).astype(BF16)

    dd = DIFF_HEADS * LANES
    p = jnp.dot(hb, win_ref[:, 2 * dg:], preferred_element_type=F32)
    scale = DIFF_QK_DIM ** -0.5 * LOG2E
    for h in range(DIFF_HEADS):
        q = p[:, h * LANES:(h + 1) * LANES] * scale
        k = p[:, dd + h * LANES:dd + (h + 1) * LANES]
        vv = p[:, 2 * dd + h * LANES:2 * dd + (h + 1) * LANES]
        k_ref[h] = k.astype(BF16)
        qt = q.T.astype(BF16)
        vt = vv.T.astype(BF16)
        for t in range(tm // ATT_TILE):
            qt_ref[h, t] = qt[:, t * ATT_TILE:(t + 1) * ATT_TILE]
            vt_ref[h, t] = vt[:, t * ATT_TILE:(t + 1) * ATT_TILE]


def _mix_in(x2, ng, win, lng, lnb, sw, sb, og):
    n, d = x2.shape
    dg = lng.shape[1]
    tm = min(TOKEN_TILE, n)
    nt = tm // ATT_TILE
    const = lambda *shape: pl.BlockSpec(shape, lambda i: (0,) * len(shape))
    return pl.pallas_call(
        _mix_in_kernel,
        grid=(n // tm,),
        in_specs=[
            pl.BlockSpec((tm, d), lambda i: (i, 0)),
            const(1, d),
            const(*win.shape),
            const(1, dg),
            const(1, dg),
            const(*sw.shape),
            const(*sb.shape),
            const(1, dg),
        ],
        out_specs=[
            pl.BlockSpec((tm, dg), lambda i: (i, 0)),
            pl.BlockSpec((DIFF_HEADS, nt, LANES, ATT_TILE), lambda i: (0, i, 0, 0)),
            pl.BlockSpec((DIFF_HEADS, tm, LANES), lambda i: (0, i, 0)),
            pl.BlockSpec((DIFF_HEADS, nt, LANES, ATT_TILE), lambda i: (0, i, 0, 0)),
        ],
        out_shape=[
            jax.ShapeDtypeStruct((n, dg), BF16),
            jax.ShapeDtypeStruct((DIFF_HEADS, n // ATT_TILE, LANES, ATT_TILE), BF16),
            jax.ShapeDtypeStruct((DIFF_HEADS, n, LANES), BF16),
            jax.ShapeDtypeStruct((DIFF_HEADS, n // ATT_TILE, LANES, ATT_TILE), BF16),
        ],
        compiler_params=pltpu.CompilerParams(
            dimension_semantics=("arbitrary",), vmem_limit_bytes=VMEM_LIMIT),
        name="mix_in",
    )(x2, ng, win, lng, lnb, sw, sb, og)


def _attn_kernel(lq1_ref, lk1_ref, lq2_ref, lk2_ref, qt_ref, k_ref, vt_ref, bias_ref, g_ref,
                 o_ref, acc_ref, m_ref, l_ref, qwa_ref, qwb_ref, s0_ref, s1_ref, cm0_ref, cm1_ref, *, lam_init):
    pi = pl.program_id(2)
    t = ATT_TILE
    qa = 2 * pi
    qb = qa + 1
    b0, b1 = (s0_ref, cm0_ref), (s1_ref, cm1_ref)

    for n, qw_ref in enumerate((qwa_ref, qwb_ref)):
        qt = qt_ref[n]
        drow = lax.broadcasted_iota(jnp.int32, qt.shape, 0)
        zero = jnp.zeros_like(qt)
        qw_ref[:, :t] = jnp.where(drow < DIFF_QK_DIM, qt, zero)
        qw_ref[:, t:] = jnp.where(drow >= DIFF_QK_DIM, qt, zero)

    def produce(j, buf, kind, qw_ref):
        s_ref, cm_ref = buf
        k = k_ref[pl.ds(pl.multiple_of(j * t, t), t), :]
        s = jnp.dot(k, qw_ref[...], preferred_element_type=F32)
        if kind != "far":
            bias = bias_ref[0] if kind == "diag" else bias_ref[1]
            halves = [s[:, :t] + bias, s[:, t:] + bias]
            if kind == "diag":
                kc = lax.broadcasted_iota(jnp.int32, (t, t), 0) // CHUNK
                qc = lax.broadcasted_iota(jnp.int32, (t, t), 1) // CHUNK
                halves = [jnp.where(kc <= qc, h, NEG_INF) for h in halves]
            s = jnp.concatenate(halves, axis=1)
        s_ref[...] = s
        cm_ref[...] = jnp.max(s, axis=0, keepdims=True)

    def consume(j, buf):
        s_ref, cm_ref = buf
        m_old = m_ref[...]
        m_new = jnp.maximum(m_old, cm_ref[...])
        alpha = jnp.exp2(m_old - m_new)
        p = jnp.exp2(s_ref[...] - m_new)
        l_ref[...] = alpha * l_ref[...] + jnp.sum(p, axis=0, keepdims=True)
        m_ref[...] = m_new
        acc_ref[...] = alpha * acc_ref[...] + jnp.dot(vt_ref[j], p.astype(BF16), preferred_element_type=F32)

    def step(j, cbuf, pbuf, kind, qw_ref):
        produce(j + 1, pbuf, kind, qw_ref)
        consume(j, cbuf)

    def far_pairs(first, count, qw_ref):
        def body(i, carry):
            j = first + 2 * i
            step(j, b0, b1, "far", qw_ref)
            step(j + 1, b1, b0, "far", qw_ref)
            return carry
        lax.fori_loop(0, count, body, 0)

    def init():
        acc_ref[...] = jnp.zeros_like(acc_ref)
        m_ref[...] = jnp.full_like(m_ref, NEG_INF)
        l_ref[...] = jnp.zeros_like(l_ref)

    def finalize(row):
        lam = (jnp.exp(jnp.sum(lq1_ref[...] * lk1_ref[...], keepdims=True))
               - jnp.exp(jnp.sum(lq2_ref[...] * lk2_ref[...], keepdims=True)) + lam_init)
        on = acc_ref[...] * (1.0 / l_ref[...])
        o = on[:, :t] - lam * on[:, t:]
        o = o * lax.rsqrt(jnp.mean(o * o, axis=0, keepdims=True) + NORM_EPS)
        o_ref[row:row + t, :] = (o.T * (g_ref[...] * (1.0 - lam_init))).astype(BF16)

    init()

    @pl.when(pi == 0)
    def _():
        produce(0, b0, "diag", qwa_ref)

    @pl.when(pi >= 1)
    def _():
        produce(0, b0, "far", qwa_ref)

    far_pairs(0, jnp.maximum(pi - 1, 0), qwa_ref)

    @pl.when(pi >= 1)
    def _():
        step(qa - 2, b0, b1, "sub", qwa_ref)
        step(qa - 1, b1, b0, "diag", qwa_ref)

    @pl.when(pi == 0)
    def _():
        produce(0, b1, "sub", qwb_ref)
        consume(qa, b0)

    @pl.when(pi >= 1)
    def _():
        produce(0, b1, "far", qwb_ref)
        consume(qa, b0)

    finalize(0)
    init()

    @pl.when(pi >= 1)
    def _():
        step(0, b1, b0, "far", qwb_ref)

    far_pairs(1, jnp.maximum(pi - 1, 0), qwb_ref)

    @pl.when(pi >= 1)
    def _():
        step(qb - 2, b0, b1, "sub", qwb_ref)

    step(qb - 1, b1, b0, "diag", qwb_ref)
    consume(qb, b0)
    finalize(t)


def _diff_attn(qt, k, vt, bias, lams, subln_g, batch, seq, lam_init):
    t = ATT_TILE
    nq = seq // t
    assert nq % 2 == 0
    npair = nq // 2
    lam_spec = pl.BlockSpec((1, DIFF_QK_DIM), lambda b, h, i: (0, 0))
    return pl.pallas_call(
        functools.partial(_attn_kernel, lam_init=lam_init),
        grid=(batch, DIFF_HEADS, npair),
        in_specs=[
            lam_spec, lam_spec, lam_spec, lam_spec,
            pl.BlockSpec((None, 2, LANES, t), lambda b, h, i: (h, b * npair + i, 0, 0)),
            pl.BlockSpec((None, seq, LANES), lambda b, h, i: (h, b, 0)),
            pl.BlockSpec((None, nq, LANES, t), lambda b, h, i: (h, b, 0, 0)),
            pl.BlockSpec((None, 2, t, t), lambda b, h, i: (h, 0, 0, 0)),
            pl.BlockSpec((1, DIFF_V_DIM), lambda b, h, i: (0, 0)),
        ],
        out_specs=pl.BlockSpec((2 * t, DIFF_V_DIM), lambda b, h, i: (b * npair + i, h)),
        out_shape=jax.ShapeDtypeStruct((batch * seq, DIFF_HEADS * DIFF_V_DIM), BF16),
        scratch_shapes=[
            pltpu.VMEM((DIFF_V_DIM, 2 * t), F32),
            pltpu.VMEM((1, 2 * t), F32),
            pltpu.VMEM((1, 2 * t), F32),
            pltpu.VMEM((LANES, 2 * t), BF16),
            pltpu.VMEM((LANES, 2 * t), BF16),
            pltpu.VMEM((t, 2 * t), F32),
            pltpu.VMEM((t, 2 * t), F32),
            pltpu.VMEM((1, 2 * t), F32),
            pltpu.VMEM((1, 2 * t), F32),
        ],
        compiler_params=pltpu.CompilerParams(
            dimension_semantics=("arbitrary", "arbitrary", "arbitrary"), vmem_limit_bytes=VMEM_LIMIT),
        name="diff_attn",
    )(*lams, qt, k, vt, bias, subln_g)


def _rel_bucket(rel):
    nb = REL_BUCKETS // 2
    max_exact = nb // 2
    side = jnp.where(rel > 0, nb, 0)
    n = jnp.abs(rel)
    nf = jnp.maximum(n, 1).astype(F32)
    large = max_exact + (jnp.log(nf / max_exact) / math.log(REL_MAX_DIST / max_exact)
                         * (nb - max_exact)).astype(jnp.int32)
    large = jnp.minimum(large, nb - 1)
    return side + jnp.where(n < max_exact, n, large)


def _near_bias_tiles(rel_table):
    t = ATT_TILE
    kpos = jnp.arange(t)[:, None]
    qpos = jnp.arange(t)[None, :]
    rel = jnp.stack([kpos - qpos, kpos - qpos - t])
    onehot = _rel_bucket(rel)[..., None] == jnp.arange(REL_BUCKETS)
    table = rel_table.astype(F32)
    far = table[REL_BUCKETS // 2 - 1]
    tiles = [jnp.sum(jnp.where(onehot, table[:, h], 0.0), axis=-1) - far[h] for h in range(DIFF_HEADS)]
    return jnp.stack(tiles) * LOG2E


def _mix_out_kernel(x_ref, g_ref, d_ref, wo_ref, ng_ref, rwt_ref, rb_ref,
                    x1_ref, h2_ref, gate_ref, lpos_ref, lstart_ref, lcnt_ref, gbase_ref, cnt_ref, base_ref):
    i = pl.program_id(0)
    tm = x_ref.shape[0]
    dg = g_ref.shape[1]

    @pl.when(i == 0)
    def _():
        base_ref[...] = jnp.zeros_like(base_ref)

    mix = (jnp.dot(g_ref[...], wo_ref[:dg, :], preferred_element_type=F32)
           + jnp.dot(d_ref[...], wo_ref[dg:, :], preferred_element_type=F32))
    x1 = x_ref[...] + mix
    x1_ref[...] = x1
    h2 = _rms(x1, ng_ref[...])
    h_hi = h2.astype(BF16)
    h2_ref[...] = h_hi

    h_lo = (h2 - h_hi.astype(F32)).astype(BF16)
    rw = rwt_ref[...]
    r_hi = rw.astype(BF16)
    r_lo = (rw - r_hi.astype(F32)).astype(BF16)
    nt_dot = lambda a, b: lax.dot_general(a, b, (((1,), (1,)), ((), ())), preferred_element_type=F32)
    logits = nt_dot(r_hi, h_hi) + (nt_dot(r_hi, h_lo) + nt_dot(r_lo, h_hi)) + rb_ref[...]
    eidx = lax.broadcasted_iota(jnp.int32, logits.shape, 0)
    cur = logits
    vals, sels = [], []
    for k in range(TOP_K):
        mx = jnp.max(cur, axis=0, keepdims=True)
        idx = jnp.min(jnp.where(cur == mx, eidx, N_EXPERTS), axis=0, keepdims=True)
        sel = eidx == idx
        vals.append(mx)
        sels.append(sel)
        cur = jnp.where(sel, -jnp.inf, cur)
    ex = [jnp.exp(v - vals[0]) for v in vals]
    inv = 1.0 / (ex[0] + ex[1] + ex[2] + ex[3])
    for k in range(TOP_K):
        gate_ref[k:k + 1, :] = ex[k] * inv

    member = sels[0] | sels[1] | sels[2] | sels[3]
    mem_b = jnp.where(member, 1.0, 0.0).astype(BF16)
    ri = lax.broadcasted_iota(jnp.int32, (tm, tm), 0)
    ci = lax.broadcasted_iota(jnp.int32, (tm, tm), 1)
    tri = jnp.where(ri < ci, 1.0, 0.0).astype(BF16)
    excl = jnp.dot(mem_b, tri, preferred_element_type=F32)
    er = lax.broadcasted_iota(jnp.int32, (N_EXPERTS, N_EXPERTS), 0)
    ec = lax.broadcasted_iota(jnp.int32, (N_EXPERTS, N_EXPERTS), 1)
    lower = jnp.where(ec < er, 1.0, 0.0).astype(BF16)
    lcnt = jnp.floor((jnp.sum(mem_b.astype(F32), axis=1, keepdims=True) + (SUBLANES - 1)) * (1.0 / SUBLANES))
    groups = jnp.broadcast_to(lcnt, (N_EXPERTS, LANES)).astype(BF16)
    lstart = jnp.dot(lower, groups, preferred_element_type=F32)[:, :1] * SUBLANES
    lcnt = lcnt * SUBLANES
    lrank = lstart + excl
    for k in range(TOP_K):
        lpos_ref[k:k + 1, :] = jnp.sum(jnp.where(sels[k], lrank, 0.0), axis=0, keepdims=True).astype(jnp.int32)
    gbase = base_ref[...]
    lstart_ref[...] = jnp.broadcast_to(lstart, lstart_ref.shape).astype(jnp.int32)
    lcnt_ref[...] = jnp.broadcast_to(lcnt, lcnt_ref.shape).astype(jnp.int32)
    gbase_ref[...] = jnp.broadcast_to(gbase, gbase_ref.shape).astype(jnp.int32)
    base_ref[...] = gbase + lcnt
    cnt_ref[...] = jnp.broadcast_to(gbase + lcnt, cnt_ref.shape).astype(jnp.int32)


def _mix_out(x2, gout, dout, wo, ng, rwt, rb):
    n, d = x2.shape
    dg = gout.shape[1]
    tm = min(TOKEN_TILE, n)
    nt = n // tm
    const = lambda *shape: pl.BlockSpec(shape, lambda i: (0,) * len(shape))
    tok = lambda w: pl.BlockSpec((tm, w), lambda i: (i, 0))
    slot = pl.BlockSpec((TOP_K, tm), lambda i: (0, i))
    seg = pl.BlockSpec((N_EXPERTS, LANES), lambda i: (i, 0))
    seg_shape = jax.ShapeDtypeStruct((nt * N_EXPERTS, LANES), jnp.int32)
    return pl.pallas_call(
        _mix_out_kernel,
        grid=(nt,),
        in_specs=[tok(d), tok(dg), tok(dout.shape[1]), const(*wo.shape), const(1, d),
                  const(*rwt.shape), const(N_EXPERTS, 1)],
        out_specs=[tok(d), tok(d), slot, slot, seg, seg, seg, const(N_EXPERTS, LANES)],
        out_shape=[
            jax.ShapeDtypeStruct((n, d), F32),
            jax.ShapeDtypeStruct((n, d), BF16),
            jax.ShapeDtypeStruct((TOP_K, n), F32),
            jax.ShapeDtypeStruct((TOP_K, n), jnp.int32),
            seg_shape, seg_shape, seg_shape,
            jax.ShapeDtypeStruct((N_EXPERTS, LANES), jnp.int32),
        ],
        scratch_shapes=[pltpu.VMEM((N_EXPERTS, 1), F32)],
        compiler_params=pltpu.CompilerParams(
            dimension_semantics=("arbitrary",), vmem_limit_bytes=VMEM_LIMIT),
        name="mix_out",
    )(x2, gout, dout, wo, ng, rwt, rb)


def _stage_rows(tm):
    return pl.cdiv(TOP_K * tm + N_EXPERTS * (SUBLANES - 1), MXU_WIDTH) * MXU_WIDTH


def _segment_copies(src_ref, src_row, dst_ref, dst_row, length, max_len, sem, start):
    size = SUBLANES
    while size <= max_len:
        off = length & (size - 1)

        @pl.when((length & size) != 0)
        def _(size=size, off=off):
            cp = pltpu.make_async_copy(src_ref.at[pl.ds(pl.multiple_of(src_row + off, SUBLANES), size), :],
                                       dst_ref.at[pl.ds(pl.multiple_of(dst_row + off, SUBLANES), size), :], sem)
            cp.start() if start else cp.wait()
        size *= 2


def _pack_bf16_pairs(x):
    c = x.shape[1] // 2
    xb = x.astype(BF16).astype(F32)
    lo = lax.shift_right_logical(lax.bitcast_convert_type(xb[:, :c], jnp.uint32), jnp.uint32(16))
    hi = lax.bitcast_convert_type(xb[:, c:], jnp.uint32) & jnp.uint32(0xFFFF0000)
    return hi | lo


def _unpack_bf16_pairs(w):
    lo = lax.bitcast_convert_type(lax.shift_left(w, jnp.uint32(16)), F32)
    hi = lax.bitcast_convert_type(w & jnp.uint32(0xFFFF0000), F32)
    return jnp.concatenate([lo, hi], axis=1).astype(BF16)


def _dispatch_kernel(cnt_ref, pstart_ref, padded_ref, ssrc_ref, slen_ref, sdst_ref,
                     lpos_ref, h_ref, xs_ref, stage_ref, zero_ref, sem, zsem):
    i = pl.program_id(0)
    tm = h_ref.shape[0]
    zr = zero_ref.shape[0]

    @pl.when(i == 0)
    def _():
        zero_ref[...] = jnp.zeros_like(zero_ref)
        used = pstart_ref[N_EXPERTS - 1] + padded_ref[N_EXPERTS - 1]

        def pad_rows(start):
            def per_expert(e, carry):
                _segment_copies(zero_ref, 0, xs_ref, pstart_ref[e] + cnt_ref[e], padded_ref[e] - cnt_ref[e],
                                zr, zsem, start)
                return carry
            lax.fori_loop(0, N_EXPERTS, per_expert, 0)

            def per_tail(r, carry):
                cp = pltpu.make_async_copy(
                    zero_ref, xs_ref.at[pl.ds(pl.multiple_of(used + r * zr, zr), zr), :], zsem)
                cp.start() if start else cp.wait()
                return carry
            lax.fori_loop(0, (xs_ref.shape[0] - used) // zr, per_tail, 0)

        pad_rows(True)
        pad_rows(False)

    slot = i & 1
    stage = stage_ref.at[slot]
    h = h_ref[...]
    chunk = MXU_WIDTH
    lp = [lpos_ref[k:k + 1, :] for k in range(TOP_K)]
    for c in range(stage.shape[0] // chunk):
        r = lax.broadcasted_iota(jnp.int32, (chunk, tm), 0) + c * chunk
        hit = (r == lp[0]) | (r == lp[1]) | (r == lp[2]) | (r == lp[3])
        perm = jnp.where(hit, 1.0, 0.0).astype(BF16)
        rows = jnp.dot(perm, h, preferred_element_type=F32)
        stage[c * chunk:(c + 1) * chunk, :] = _pack_bf16_pairs(rows)

    def segments(tile, tile_slot, start):
        def per_segment(e, carry):
            s = tile * N_EXPERTS + e
            _segment_copies(stage_ref.at[tile_slot], ssrc_ref[s], xs_ref, sdst_ref[s], slen_ref[s], tm,
                            sem.at[tile_slot], start)
            return carry
        lax.fori_loop(0, N_EXPERTS, per_segment, 0)

    segments(i, slot, True)

    @pl.when(i > 0)
    def _():
        segments(i - 1, 1 - slot, False)

    @pl.when(i == pl.num_programs(0) - 1)
    def _():
        segments(i, slot, False)


def _dispatch(counts, pad_start, padded, seg_src, seg_len, seg_dst, lpos, h2, cap):
    n, d = h2.shape
    tm = min(TOKEN_TILE, n)
    return pl.pallas_call(
        _dispatch_kernel,
        grid_spec=pltpu.PrefetchScalarGridSpec(
            num_scalar_prefetch=6,
            grid=(n // tm,),
            in_specs=[
                pl.BlockSpec((TOP_K, tm), lambda i, *_: (0, i)),
                pl.BlockSpec((tm, d), lambda i, *_: (i, 0)),
            ],
            out_specs=pl.BlockSpec(memory_space=pl.ANY),
            scratch_shapes=[pltpu.VMEM((2, _stage_rows(tm), d // 2), jnp.uint32),
                            pltpu.VMEM((MOE_BLOCK, d // 2), jnp.uint32),
                            pltpu.SemaphoreType.DMA((2,)), pltpu.SemaphoreType.DMA],
        ),
        out_shape=jax.ShapeDtypeStruct((cap, d // 2), jnp.uint32),
        compiler_params=pltpu.CompilerParams(
            dimension_semantics=("arbitrary",), vmem_limit_bytes=VMEM_LIMIT),
        name="dispatch",
    )(counts, pad_start, padded, seg_src, seg_len, seg_dst, lpos, h2)


def _expert_kernel(pstart_ref, padded_ref, nact_ref, xs_ref, w1_ref, b1_ref, w2_ref, b2_ref, y_ref,
                   w1p_ref, w2b_ref, xbuf_ref, ybuf_ref, xsem, ysem):
    e = pl.program_id(0)
    blk = xbuf_ref.shape[1]
    n_active = nact_ref[0]
    first = pstart_ref[e] // blk
    count = padded_ref[e] // blk
    gw = MXU_WIDTH
    half = gw // 2

    def x_copy(g, slot):
        return pltpu.make_async_copy(xs_ref.at[pl.ds(pl.multiple_of(g * blk, blk), blk), :],
                                     xbuf_ref.at[slot], xsem.at[slot])

    def y_copy(g, slot):
        return pltpu.make_async_copy(ybuf_ref.at[slot],
                                     y_ref.at[pl.ds(pl.multiple_of(g * blk, blk), blk), :], ysem.at[slot])

    @pl.when(jnp.logical_and(e == 0, n_active > 0))
    def _():
        x_copy(0, 0).start()

    @pl.when(count > 0)
    def _():
        r = lax.broadcasted_iota(jnp.int32, (gw, gw), 0)
        c = lax.broadcasted_iota(jnp.int32, (gw, gw), 1)
        perm = jnp.where(r == jnp.where(c < half, 2 * c, 2 * (c - half) + 1), 1.0, 0.0).astype(BF16)
        for j in range(w1_ref.shape[1] // gw):
            w = w1_ref[:, j * gw:(j + 1) * gw].astype(BF16)
            w1p_ref[:, j * gw:(j + 1) * gw] = jnp.dot(w, perm, preferred_element_type=F32).astype(BF16)
        w2b_ref[...] = w2_ref[...].astype(BF16)

    def block(i, carry):
        g = first + i
        slot = g & 1
        x_copy(g, slot).wait()

        @pl.when(g + 1 < n_active)
        def _():
            x_copy(g + 1, 1 - slot).start()

        @pl.when(g >= 2)
        def _():
            y_copy(g - 2, slot).wait()

        x = _unpack_bf16_pairs(xbuf_ref[slot])
        a = jnp.dot(x, w1p_ref[...], preferred_element_type=F32) + b1_ref[...]
        acts = []
        for j in range(a.shape[1] // gw):
            ag = jnp.minimum(a[:, j * gw:j * gw + half], SWIGLU_LIMIT)
            al = jnp.clip(a[:, j * gw + half:(j + 1) * gw], -SWIGLU_LIMIT, SWIGLU_LIMIT)
            acts.append((ag * (1.0 / (1.0 + jnp.exp(-SWIGLU_ALPHA * ag))) * (al + 1.0)).astype(BF16))
        act = jnp.concatenate(acts, axis=1)
        y = jnp.dot(act, w2b_ref[...], preferred_element_type=F32) + b2_ref[...]
        ybuf_ref[slot] = _pack_bf16_pairs(y)
        y_copy(g, slot).start()
        return carry

    lax.fori_loop(0, count, block, 0)

    @pl.when(e == pl.num_programs(0) - 1)
    def _():
        for back in (2, 1):
            @pl.when(n_active >= back)
            def _(back=back):
                y_copy(n_active - back, (n_active - back) & 1).wait()

        xbuf_ref[0] = jnp.zeros_like(xbuf_ref[0])
        n_tail = y_ref.shape[0] // blk - n_active

        def tail(start):
            def body(i, carry):
                cp = pltpu.make_async_copy(
                    xbuf_ref.at[0], y_ref.at[pl.ds(pl.multiple_of((n_active + i) * blk, blk), blk), :], xsem.at[0])
                cp.start() if start else cp.wait()
                return carry
            lax.fori_loop(0, n_tail, body, 0)

        tail(True)
        tail(False)


def _experts(pad_start, padded, nact, xs, w1, b1p, w2, b2):
    cap, dh = xs.shape
    d = 2 * dh
    de2 = w1.shape[2]
    de = w2.shape[1]
    exp3 = lambda e, *_: (e, 0, 0)
    return pl.pallas_call(
        _expert_kernel,
        grid_spec=pltpu.PrefetchScalarGridSpec(
            num_scalar_prefetch=3,
            grid=(N_EXPERTS,),
            in_specs=[
                pl.BlockSpec(memory_space=pl.ANY),
                pl.BlockSpec((None, d, de2), exp3),
                pl.BlockSpec((None, 1, de2), exp3),
                pl.BlockSpec((None, de, d), exp3),
                pl.BlockSpec((None, 1, d), exp3),
            ],
            out_specs=pl.BlockSpec(memory_space=pl.ANY),
            scratch_shapes=[pltpu.VMEM((d, de2), BF16), pltpu.VMEM((de, d), BF16),
                            pltpu.VMEM((2, MOE_BLOCK, dh), jnp.uint32), pltpu.VMEM((2, MOE_BLOCK, dh), jnp.uint32),
                            pltpu.SemaphoreType.DMA((2,)), pltpu.SemaphoreType.DMA((2,))],
        ),
        out_shape=jax.ShapeDtypeStruct((cap, dh), jnp.uint32),
        compiler_params=pltpu.CompilerParams(
            dimension_semantics=("arbitrary",), vmem_limit_bytes=VMEM_LIMIT),
        name="experts",
    )(pad_start, padded, nact, xs, w1, b1p, w2, b2)


def _combine_kernel(ssrc_ref, slen_ref, sdst_ref, lpos_ref, gate_ref, y_ref, x1_ref, fg_ref, o_ref,
                    stage_ref, g_ref, sem):
    i = pl.program_id(0)
    tm = x1_ref.shape[0]

    slot = i & 1

    def segments(tile, tile_slot, start):
        def per_segment(e, carry):
            s = tile * N_EXPERTS + e
            _segment_copies(y_ref, sdst_ref[s], stage_ref.at[tile_slot], ssrc_ref[s], slen_ref[s], tm,
                            sem.at[tile_slot], start)
            return carry
        lax.fori_loop(0, N_EXPERTS, per_segment, 0)

    @pl.when(i == 0)
    def _():
        stage_ref[...] = jnp.zeros_like(stage_ref)
        segments(0, 0, True)

    @pl.when(i + 1 < pl.num_programs(0))
    def _():
        segments(i + 1, 1 - slot, True)

    chunk = MXU_WIDTH
    lp = [lpos_ref[:, k:k + 1] for k in range(TOP_K)]
    gt = [gate_ref[:, k:k + 1] for k in range(TOP_K)]
    for c in range(stage_ref.shape[1] // chunk):
        r = lax.broadcasted_iota(jnp.int32, (tm, chunk), 1) + c * chunk
        g = jnp.zeros((tm, chunk), F32)
        for k in range(TOP_K):
            g = jnp.where(r == lp[k], gt[k], g)
        g_ref[:, c * chunk:(c + 1) * chunk] = g.astype(BF16)

    segments(i, slot, False)
    y = _unpack_bf16_pairs(stage_ref[slot])
    x = x1_ref[...] + jnp.dot(g_ref[...], y, preferred_element_type=F32)
    o_ref[...] = _rms(x, fg_ref[...])


def _combine(seg_src, seg_len, seg_dst, lpos_col, gate_col, y, x1, fg):
    n, d = x1.shape
    tm = min(TOKEN_TILE, n)
    return pl.pallas_call(
        _combine_kernel,
        grid_spec=pltpu.PrefetchScalarGridSpec(
            num_scalar_prefetch=3,
            grid=(n // tm,),
            in_specs=[
                pl.BlockSpec((tm, TOP_K), lambda i, *_: (i, 0)),
                pl.BlockSpec((tm, TOP_K), lambda i, *_: (i, 0)),
                pl.BlockSpec(memory_space=pl.ANY),
                pl.BlockSpec((tm, d), lambda i, *_: (i, 0)),
                pl.BlockSpec((1, d), lambda i, *_: (0, 0)),
            ],
            out_specs=pl.BlockSpec((tm, d), lambda i, *_: (i, 0)),
            scratch_shapes=[pltpu.VMEM((2, _stage_rows(tm), d // 2), jnp.uint32),
                            pltpu.VMEM((tm, _stage_rows(tm)), BF16),
                            pltpu.SemaphoreType.DMA((2,))],
        ),
        out_shape=jax.ShapeDtypeStruct((n, d), F32),
        compiler_params=pltpu.CompilerParams(
            dimension_semantics=("arbitrary",), vmem_limit_bytes=VMEM_LIMIT),
        name="combine",
    )(seg_src, seg_len, seg_dst, lpos_col, gate_col, y, x1, fg)


def _layer(x2, batch, seq, layer, lam_init, rel_bias_table, mix_norm_g, w_in, sgu_ln_g, sgu_ln_b, sgu_w,
           sgu_b, gmlp_out_g, lambda_q1, lambda_k1, lambda_q2, lambda_k2, diff_subln_g, w_out,
           ffn_norm_g, router_w, router_b, w_mlp1, b_mlp1, w_mlp2, b_mlp2):
    n, d = x2.shape
    l = layer
    row = lambda a: a[l].reshape(1, -1).astype(F32)

    gout, qt, k, vt = _mix_in(x2, row(mix_norm_g), w_in[l].astype(BF16), row(sgu_ln_g), row(sgu_ln_b),
                              sgu_w[l].astype(F32), sgu_b[l].astype(F32)[:, :, None], row(gmlp_out_g))
    dout = _diff_attn(qt, k, vt, _near_bias_tiles(rel_bias_table),
                      (row(lambda_q1), row(lambda_k1), row(lambda_q2), row(lambda_k2)),
                      row(diff_subln_g), batch, seq, lam_init)
    x1, h2, gate, lpos, lstart, lcnt, gbase, counts = _mix_out(
        x2, gout, dout, w_out[l].astype(BF16), row(ffn_norm_g),
        router_w[l].astype(F32).T, router_b[l].astype(F32)[:, None])

    counts = counts[:, 0]
    padded = (counts + MOE_BLOCK - 1) // MOE_BLOCK * MOE_BLOCK
    pad_end = jnp.cumsum(padded)
    pad_start = pad_end - padded
    n_tiles = lstart.shape[0] // N_EXPERTS
    nblk = pl.cdiv(n * TOP_K + n_tiles * N_EXPERTS * (SUBLANES - 1), MOE_BLOCK) + N_EXPERTS
    cap = nblk * MOE_BLOCK
    n_active = (pad_end[-1] // MOE_BLOCK).astype(jnp.int32).reshape(1)
    seg_src = lstart[:, 0]
    seg_len = lcnt[:, 0]
    seg_dst = gbase[:, 0] + jnp.tile(pad_start, n_tiles)

    xs = _dispatch(counts, pad_start, padded, seg_src, seg_len, seg_dst, lpos, h2, cap)
    half = MXU_WIDTH // 2
    b1p = b_mlp1[l].astype(F32).reshape(N_EXPERTS, -1, half, 2).transpose(0, 1, 3, 2).reshape(N_EXPERTS, 1, -1)
    y = _experts(pad_start, padded, n_active, xs, w_mlp1[l].astype(F32), b1p,
                 w_mlp2[l].astype(F32), b_mlp2[l][:, None, :].astype(F32))
    return (seg_src, seg_len, seg_dst, lpos.T, gate.T), y, x1


def kernel(x, rel_bias_table, mix_norm_g, w_in, sgu_ln_g, sgu_ln_b, sgu_w, sgu_b, gmlp_out_g, lambda_q1,
           lambda_k1, lambda_q2, lambda_k2, diff_subln_g, w_out, ffn_norm_g, router_w, router_b, w_mlp1,
           b_mlp1, w_mlp2, b_mlp2, final_norm_g):
    batch, seq, d = x.shape
    depth = w_in.shape[0]
    assert depth == 1, "the final RMSNorm is fused into the last layer's combine step"
    assert seq % (2 * ATT_TILE) == 0 and ATT_TILE == TOKEN_TILE
    x2 = x.reshape(batch * seq, d).astype(F32)
    lam_init = 0.8 - 0.6 * math.exp(-0.3 * 0)
    routing, y, x1 = _layer(
        x2, batch, seq, 0, lam_init, rel_bias_table, mix_norm_g, w_in, sgu_ln_g, sgu_ln_b, sgu_w, sgu_b,
        gmlp_out_g, lambda_q1, lambda_k1, lambda_q2, lambda_k2, diff_subln_g, w_out, ffn_norm_g,
        router_w, router_b, w_mlp1, b_mlp1, w_mlp2, b_mlp2)
    out = _combine(*routing, y, x1, final_norm_g.reshape(1, d).astype(F32))
    return out.reshape(batch, seq, d).astype(x.dtype)
```

```python
import functools
import math

import jax
import jax.numpy as jnp
from jax import lax
from jax.experimental import pallas as pl
from jax.experimental.pallas import tpu as pltpu

F32 = jnp.float32
BF16 = jnp.bfloat16

CHUNK = 64
GMLP_GROUPS = 4
SGU_BLOCK = 128
DIFF_HEADS = 4
DIFF_QK_DIM = 64
DIFF_V_DIM = 128
REL_BUCKETS = 32
REL_MAX_DIST = 128
N_EXPERTS = 32
TOP_K = 4
SWIGLU_ALPHA = 1.702
SWIGLU_LIMIT = 7.0
NORM_EPS = 1e-5
NEG_INF = -1e30

LANES = 128
SUBLANES = 8
MXU_WIDTH = 256
ATT_TILE = 512
LOG2E = 1.4426950408889634
TOKEN_TILE = 512
MOE_BLOCK = 512
VMEM_LIMIT = 56 * 1024 * 1024


def _rms(x, g):
    return x * lax.rsqrt(jnp.mean(x * x, axis=-1, keepdims=True) + NORM_EPS) * g


def _gelu(z):
    return 0.5 * z * (1.0 + lax.erf(z * (1.0 / math.sqrt(2.0))))


def _mix_in_kernel(x_ref, ng_ref, win_ref, lng_ref, lnb_ref, sw_ref, sb_ref, og_ref,
                   gout_ref, qt_ref, k_ref, vt_ref):
    tm = x_ref.shape[0]
    dg = gout_ref.shape[1]
    gw = dg // GMLP_GROUPS
    hb = _rms(x_ref[...], ng_ref[...]).astype(BF16)

    z = _gelu(jnp.dot(hb, win_ref[:, :2 * dg], preferred_element_type=F32))
    u = z[:, :dg]
    v = z[:, dg:]
    mu = jnp.mean(v, axis=-1, keepdims=True)
    vc = v - mu
    v = vc * lax.rsqrt(jnp.mean(vc * vc, axis=-1, keepdims=True) + NORM_EPS) * lng_ref[...] + lnb_ref[...]
    vb = v.astype(BF16)

    pos_i = lax.broadcasted_iota(jnp.int32, (SGU_BLOCK, SGU_BLOCK), 0) // CHUNK
    pos_j = lax.broadcasted_iota(jnp.int32, (SGU_BLOCK, SGU_BLOCK), 1) // CHUNK
    causal = pos_j <= pos_i
    rows = []
    for n in range(tm // SGU_BLOCK):
        cols = []
        for g in range(GMLP_GROUPS):
            w = jnp.where(causal, sw_ref[g], 0.0).astype(BF16)
            blk = vb[n * SGU_BLOCK:(n + 1) * SGU_BLOCK, g * gw:(g + 1) * gw]
            cols.append(jnp.dot(w, blk, preferred_element_type=F32) + sb_ref[g])
        rows.append(jnp.concatenate(cols, axis=1))
    sv = jnp.concatenate(rows, axis=0)
    gout_ref[...] = _rms(u * sv, og_ref[...]).astype(BF16)

    dd = DIFF_HEADS * LANES
    p = jnp.dot(hb, win_ref[:, 2 * dg:], preferred_element_type=F32)
    scale = DIFF_QK_DIM ** -0.5 * LOG2E
    for h in range(DIFF_HEADS):
        q = p[:, h * LANES:(h + 1) * LANES] * scale
        k = p[:, dd + h * LANES:dd + (h + 1) * LANES]
        vv = p[:, 2 * dd + h * LANES:2 * dd + (h + 1) * LANES]
        k_ref[h] = k.astype(BF16)
        qt = q.T.astype(BF16)
        vt = vv.T.astype(BF16)
        for t in range(tm // ATT_TILE):
            qt_ref[h, t] = qt[:, t * ATT_TILE:(t + 1) * ATT_TILE]
            vt_ref[h, t] = vt[:, t * ATT_TILE:(t + 1) * ATT_TILE]


def _mix_in(x2, ng, win, lng, lnb, sw, sb, og):
    n, d = x2.shape
    dg = lng.shape[1]
    tm = min(TOKEN_TILE, n)
    nt = tm // ATT_TILE
    const = lambda *shape: pl.BlockSpec(shape, lambda i: (0,) * len(shape))
    return pl.pallas_call(
        _mix_in_kernel,
        grid=(n // tm,),
        in_specs=[
            pl.BlockSpec((tm, d), lambda i: (i, 0)),
            const(1, d),
            const(*win.shape),
            const(1, dg),
            const(1, dg),
            const(*sw.shape),
            const(*sb.shape),
            const(1, dg),
        ],
        out_specs=[
            pl.BlockSpec((tm, dg), lambda i: (i, 0)),
            pl.BlockSpec((DIFF_HEADS, nt, LANES, ATT_TILE), lambda i: (0, i, 0, 0)),
            pl.BlockSpec((DIFF_HEADS, tm, LANES), lambda i: (0, i, 0)),
            pl.BlockSpec((DIFF_HEADS, nt, LANES, ATT_TILE), lambda i: (0, i, 0, 0)),
        ],
        out_shape=[
            jax.ShapeDtypeStruct((n, dg), BF16),
            jax.ShapeDtypeStruct((DIFF_HEADS, n // ATT_TILE, LANES, ATT_TILE), BF16),
            jax.ShapeDtypeStruct((DIFF_HEADS, n, LANES), BF16),
            jax.ShapeDtypeStruct((DIFF_HEADS, n // ATT_TILE, LANES, ATT_TILE), BF16),
        ],
        compiler_params=pltpu.CompilerParams(
            dimension_semantics=("arbitrary",), vmem_limit_bytes=VMEM_LIMIT),
        name="mix_in",
    )(x2, ng, win, lng, lnb, sw, sb, og)


def _attn_kernel(lq1_ref, lk1_ref, lq2_ref, lk2_ref, qt_ref, k_ref, vt_ref, bias_ref, g_ref,
                 o_ref, acc_ref, m_ref, l_ref, qwa_ref, qwb_ref, s0_ref, s1_ref, cm0_ref, cm1_ref, *, lam_init):
    pi = pl.program_id(2)
    t = ATT_TILE
    qa = 2 * pi
    qb = qa + 1
    b0, b1 = (s0_ref, cm0_ref), (s1_ref, cm1_ref)

    for n, qw_ref in enumerate((qwa_ref, qwb_ref)):
        qt = qt_ref[n]
        drow = lax.broadcasted_iota(jnp.int32, qt.shape, 0)
        zero = jnp.zeros_like(qt)
        qw_ref[:, :t] = jnp.where(drow < DIFF_QK_DIM, qt, zero)
        qw_ref[:, t:] = jnp.where(drow >= DIFF_QK_DIM, qt, zero)

    def produce(j, buf, kind, qw_ref):
        s_ref, cm_ref = buf
        k = k_ref[pl.ds(pl.multiple_of(j * t, t), t), :]
        s = jnp.dot(k, qw_ref[...], preferred_element_type=F32)
        if kind == "diag":
            bias = bias_ref[0]
            kc = lax.broadcasted_iota(jnp.int32, (t, t), 0) // CHUNK
            qc = lax.broadcasted_iota(jnp.int32, (t, t), 1) // CHUNK
            s = jnp.concatenate([jnp.where(kc <= qc, h + bias, NEG_INF) for h in (s[:, :t], s[:, t:])], axis=1)
        elif kind == "sub":
            r = REL_MAX_DIST
            corner = bias_ref[1, t - r:, :r]
            low = jnp.concatenate([s[t - r:, :r] + corner, s[t - r:, r:t],
                                   s[t - r:, t:t + r] + corner, s[t - r:, t + r:]], axis=1)
            s = jnp.concatenate([s[:t - r, :], low], axis=0)
        s_ref[...] = s
        cm_ref[...] = jnp.max(s, axis=0, keepdims=True)

    def consume(j, buf):
        s_ref, cm_ref = buf
        m_old = m_ref[...]
        m_new = jnp.maximum(m_old, cm_ref[...])
        alpha = jnp.exp2(m_old - m_new)
        p = jnp.exp2(s_ref[...] - m_new)
        l_ref[...] = alpha * l_ref[...] + jnp.sum(p, axis=0, keepdims=True)
        m_ref[...] = m_new
        acc_ref[...] = alpha * acc_ref[...] + jnp.dot(vt_ref[j], p.astype(BF16), preferred_element_type=F32)

    def step(j, cbuf, pbuf, kind, qw_ref):
        produce(j + 1, pbuf, kind, qw_ref)
        consume(j, cbuf)

    def far_pairs(first, count, qw_ref):
        def pair(j):
            step(j, b0, b1, "far", qw_ref)
            step(j + 1, b1, b0, "far", qw_ref)

        def body(i, carry):
            pair(first + 4 * i)
            pair(first + 4 * i + 2)
            return carry
        lax.fori_loop(0, lax.shift_right_logical(count, 1), body, 0)

        @pl.when((count & 1) == 1)
        def _():
            pair(first + 2 * (count - 1))

    def init():
        acc_ref[...] = jnp.zeros_like(acc_ref)
        m_ref[...] = jnp.full_like(m_ref, NEG_INF)
        l_ref[...] = jnp.zeros_like(l_ref)

    def finalize(row):
        lam = (jnp.exp(jnp.sum(lq1_ref[...] * lk1_ref[...], keepdims=True))
               - jnp.exp(jnp.sum(lq2_ref[...] * lk2_ref[...], keepdims=True)) + lam_init)
        on = acc_ref[...] * (1.0 / l_ref[...])
        o = on[:, :t] - lam * on[:, t:]
        o = o * lax.rsqrt(jnp.mean(o * o, axis=0, keepdims=True) + NORM_EPS)
        o_ref[row:row + t, :] = (o.T * (g_ref[...] * (1.0 - lam_init))).astype(BF16)

    init()

    @pl.when(pi == 0)
    def _():
        produce(0, b0, "diag", qwa_ref)

    @pl.when(pi >= 1)
    def _():
        produce(0, b0, "far", qwa_ref)

    far_pairs(0, jnp.maximum(pi - 1, 0), qwa_ref)

    @pl.when(pi >= 1)
    def _():
        step(qa - 2, b0, b1, "sub", qwa_ref)
        step(qa - 1, b1, b0, "diag", qwa_ref)

    @pl.when(pi == 0)
    def _():
        produce(0, b1, "sub", qwb_ref)
        consume(qa, b0)

    @pl.when(pi >= 1)
    def _():
        produce(0, b1, "far", qwb_ref)
        consume(qa, b0)

    finalize(0)
    init()

    @pl.when(pi >= 1)
    def _():
        step(0, b1, b0, "far", qwb_ref)

    far_pairs(1, jnp.maximum(pi - 1, 0), qwb_ref)

    @pl.when(pi >= 1)
    def _():
        step(qb - 2, b0, b1, "sub", qwb_ref)

    step(qb - 1, b1, b0, "diag", qwb_ref)
    consume(qb, b0)
    finalize(t)


def _diff_attn(qt, k, vt, bias, lams, subln_g, batch, seq, lam_init):
    t = ATT_TILE
    nq = seq // t
    assert nq % 2 == 0
    npair = nq // 2
    lam_spec = pl.BlockSpec((1, DIFF_QK_DIM), lambda b, h, i: (0, 0))
    return pl.pallas_call(
        functools.partial(_attn_kernel, lam_init=lam_init),
        grid=(batch, DIFF_HEADS, npair),
        in_specs=[
            lam_spec, lam_spec, lam_spec, lam_spec,
            pl.BlockSpec((None, 2, LANES, t), lambda b, h, i: (h, b * npair + i, 0, 0)),
            pl.BlockSpec((None, seq, LANES), lambda b, h, i: (h, b, 0)),
            pl.BlockSpec((None, nq, LANES, t), lambda b, h, i: (h, b, 0, 0)),
            pl.BlockSpec((None, 2, t, t), lambda b, h, i: (h, 0, 0, 0)),
            pl.BlockSpec((1, DIFF_V_DIM), lambda b, h, i: (0, 0)),
        ],
        out_specs=pl.BlockSpec((2 * t, DIFF_V_DIM), lambda b, h, i: (b * npair + i, h)),
        out_shape=jax.ShapeDtypeStruct((batch * seq, DIFF_HEADS * DIFF_V_DIM), BF16),
        scratch_shapes=[
            pltpu.VMEM((DIFF_V_DIM, 2 * t), F32),
            pltpu.VMEM((1, 2 * t), F32),
            pltpu.VMEM((1, 2 * t), F32),
            pltpu.VMEM((LANES, 2 * t), BF16),
            pltpu.VMEM((LANES, 2 * t), BF16),
            pltpu.VMEM((t, 2 * t), F32),
            pltpu.VMEM((t, 2 * t), F32),
            pltpu.VMEM((1, 2 * t), F32),
            pltpu.VMEM((1, 2 * t), F32),
        ],
        compiler_params=pltpu.CompilerParams(
            dimension_semantics=("arbitrary", "arbitrary", "arbitrary"), vmem_limit_bytes=VMEM_LIMIT),
        name="diff_attn",
    )(*lams, qt, k, vt, bias, subln_g)


def _rel_bucket(rel):
    nb = REL_BUCKETS // 2
    max_exact = nb // 2
    side = jnp.where(rel > 0, nb, 0)
    n = jnp.abs(rel)
    nf = jnp.maximum(n, 1).astype(F32)
    large = max_exact + (jnp.log(nf / max_exact) / math.log(REL_MAX_DIST / max_exact)
                         * (nb - max_exact)).astype(jnp.int32)
    large = jnp.minimum(large, nb - 1)
    return side + jnp.where(n < max_exact, n, large)


def _near_bias_tiles(rel_table):
    t = ATT_TILE
    table = rel_table.astype(F32)
    far = table[REL_BUCKETS // 2 - 1]

    def by_rel(rel):
        return (table[_rel_bucket(rel)] - far).T * LOG2E

    def toeplitz(first_row, first_col):
        wrap = jnp.concatenate([first_row, jnp.zeros((DIFF_HEADS, 1), F32), first_col[:, :0:-1]], axis=1)
        skew = jnp.tile(wrap, (1, t))[:, :t * (2 * t - 1)].reshape(DIFF_HEADS, t, 2 * t - 1)
        return skew[:, :, :t]

    m = jnp.arange(t)
    diag = toeplitz(by_rel(-m), by_rel(m))
    sub = toeplitz(by_rel(-m - t), by_rel(m - t))
    return jnp.stack([diag, sub], axis=1)


def _mix_out_kernel(x_ref, g_ref, d_ref, wo_ref, ng_ref, rwt_ref, rb_ref,
                    x1_ref, h2_ref, gate_ref, lpos_ref, lstart_ref, lcnt_ref, gbase_ref, cnt_ref, base_ref):
    i = pl.program_id(0)
    tm = x_ref.shape[0]
    dg = g_ref.shape[1]

    @pl.when(i == 0)
    def _():
        base_ref[...] = jnp.zeros_like(base_ref)

    mix = (jnp.dot(g_ref[...], wo_ref[:dg, :], preferred_element_type=F32)
           + jnp.dot(d_ref[...], wo_ref[dg:, :], preferred_element_type=F32))
    x1 = x_ref[...] + mix
    x1_ref[...] = x1
    h2 = _rms(x1, ng_ref[...])
    h_hi = h2.astype(BF16)
    h2_ref[...] = h_hi

    h_lo = (h2 - h_hi.astype(F32)).astype(BF16)
    rw = rwt_ref[...]
    r_hi = rw.astype(BF16)
    r_lo = (rw - r_hi.astype(F32)).astype(BF16)
    nt_dot = lambda a, b: lax.dot_general(a, b, (((1,), (1,)), ((), ())), preferred_element_type=F32)
    logits = nt_dot(r_hi, h_hi) + (nt_dot(r_hi, h_lo) + nt_dot(r_lo, h_hi)) + rb_ref[...]
    eidx = lax.broadcasted_iota(jnp.int32, logits.shape, 0)
    cur = logits
    vals, sels = [], []
    for k in range(TOP_K):
        mx = jnp.max(cur, axis=0, keepdims=True)
        idx = jnp.min(jnp.where(cur == mx, eidx, N_EXPERTS), axis=0, keepdims=True)
        sel = eidx == idx
        vals.append(mx)
        sels.append(sel)
        cur = jnp.where(sel, -jnp.inf, cur)
    ex = [jnp.exp(v - vals[0]) for v in vals]
    inv = 1.0 / (ex[0] + ex[1] + ex[2] + ex[3])
    for k in range(TOP_K):
        gate_ref[k:k + 1, :] = ex[k] * inv

    member = sels[0] | sels[1] | sels[2] | sels[3]
    mem_b = jnp.where(member, 1.0, 0.0).astype(BF16)
    ri = lax.broadcasted_iota(jnp.int32, (tm, tm), 0)
    ci = lax.broadcasted_iota(jnp.int32, (tm, tm), 1)
    tri = jnp.where(ri < ci, 1.0, 0.0).astype(BF16)
    excl = jnp.dot(mem_b, tri, preferred_element_type=F32)
    er = lax.broadcasted_iota(jnp.int32, (N_EXPERTS, N_EXPERTS), 0)
    ec = lax.broadcasted_iota(jnp.int32, (N_EXPERTS, N_EXPERTS), 1)
    lower = jnp.where(ec < er, 1.0, 0.0).astype(BF16)
    lcnt = jnp.floor((jnp.sum(mem_b.astype(F32), axis=1, keepdims=True) + (SUBLANES - 1)) * (1.0 / SUBLANES))
    groups = jnp.broadcast_to(lcnt, (N_EXPERTS, LANES)).astype(BF16)
    lstart = jnp.dot(lower, groups, preferred_element_type=F32)[:, :1] * SUBLANES
    lcnt = lcnt * SUBLANES
    lrank = lstart + excl
    for k in range(TOP_K):
        lpos_ref[k:k + 1, :] = jnp.sum(jnp.where(sels[k], lrank, 0.0), axis=0, keepdims=True).astype(jnp.int32)
    gbase = base_ref[...]
    lstart_ref[...] = jnp.broadcast_to(lstart, lstart_ref.shape).astype(jnp.int32)
    lcnt_ref[...] = jnp.broadcast_to(lcnt, lcnt_ref.shape).astype(jnp.int32)
    gbase_ref[...] = jnp.broadcast_to(gbase, gbase_ref.shape).astype(jnp.int32)
    base_ref[...] = gbase + lcnt
    cnt_ref[...] = jnp.broadcast_to(gbase + lcnt, cnt_ref.shape).astype(jnp.int32)


def _mix_out(x2, gout, dout, wo, ng, rwt, rb):
    n, d = x2.shape
    dg = gout.shape[1]
    tm = min(TOKEN_TILE, n)
    nt = n // tm
    const = lambda *shape: pl.BlockSpec(shape, lambda i: (0,) * len(shape))
    tok = lambda w: pl.BlockSpec((tm, w), lambda i: (i, 0))
    slot = pl.BlockSpec((TOP_K, tm), lambda i: (0, i))
    seg = pl.BlockSpec((N_EXPERTS, LANES), lambda i: (i, 0))
    seg_shape = jax.ShapeDtypeStruct((nt * N_EXPERTS, LANES), jnp.int32)
    return pl.pallas_call(
        _mix_out_kernel,
        grid=(nt,),
        in_specs=[tok(d), tok(dg), tok(dout.shape[1]), const(*wo.shape), const(1, d),
                  const(*rwt.shape), const(N_EXPERTS, 1)],
        out_specs=[tok(d), tok(d), slot, slot, seg, seg, seg, const(N_EXPERTS, LANES)],
        out_shape=[
            jax.ShapeDtypeStruct((n, d), F32),
            jax.ShapeDtypeStruct((n, d), BF16),
            jax.ShapeDtypeStruct((TOP_K, n), F32),
            jax.ShapeDtypeStruct((TOP_K, n), jnp.int32),
            seg_shape, seg_shape, seg_shape,
            jax.ShapeDtypeStruct((N_EXPERTS, LANES), jnp.int32),
        ],
        scratch_shapes=[pltpu.VMEM((N_EXPERTS, 1), F32)],
        compiler_params=pltpu.CompilerParams(
            dimension_semantics=("arbitrary",), vmem_limit_bytes=VMEM_LIMIT),
        name="mix_out",
    )(x2, gout, dout, wo, ng, rwt, rb)


def _stage_rows(tm):
    return pl.cdiv(TOP_K * tm + N_EXPERTS * (SUBLANES - 1), MXU_WIDTH) * MXU_WIDTH


def _segment_copies(src_ref, src_row, dst_ref, dst_row, length, max_len, sem, start):
    size = SUBLANES
    while size <= max_len:
        off = length & (size - 1)

        @pl.when((length & size) != 0)
        def _(size=size, off=off):
            cp = pltpu.make_async_copy(src_ref.at[pl.ds(pl.multiple_of(src_row + off, SUBLANES), size), :],
                                       dst_ref.at[pl.ds(pl.multiple_of(dst_row + off, SUBLANES), size), :], sem)
            cp.start() if start else cp.wait()
        size *= 2


def _pack_bf16_pairs(x):
    c = x.shape[1] // 2
    return pltpu.pack_elementwise([x[:, :c], x[:, c:]], packed_dtype=BF16)


def _packed_zeros(shape):
    return _pack_bf16_pairs(jnp.zeros((shape[0], 2 * shape[1]), F32))


def _unpack_bf16_pairs(w):
    halves = [pltpu.unpack_elementwise(w, index=n, packed_dtype=BF16, unpacked_dtype=F32) for n in range(2)]
    return jnp.concatenate(halves, axis=1).astype(BF16)


def _dispatch_kernel(cnt_ref, pstart_ref, padded_ref, ssrc_ref, slen_ref, sdst_ref,
                     lpos_ref, h_ref, xs_ref, stage_ref, zero_ref, sem, zsem):
    i = pl.program_id(0)
    tm = h_ref.shape[0]
    zr = zero_ref.shape[0]

    @pl.when(i == 0)
    def _():
        zero_ref[...] = _packed_zeros(zero_ref.shape)
        used = pstart_ref[N_EXPERTS - 1] + padded_ref[N_EXPERTS - 1]

        def pad_rows(start):
            def per_expert(e, carry):
                _segment_copies(zero_ref, 0, xs_ref, pstart_ref[e] + cnt_ref[e], padded_ref[e] - cnt_ref[e],
                                zr, zsem, start)
                return carry
            lax.fori_loop(0, N_EXPERTS, per_expert, 0)

            def per_tail(r, carry):
                cp = pltpu.make_async_copy(
                    zero_ref, xs_ref.at[pl.ds(pl.multiple_of(used + r * zr, zr), zr), :], zsem)
                cp.start() if start else cp.wait()
                return carry
            lax.fori_loop(0, (xs_ref.shape[0] - used) // zr, per_tail, 0)

        pad_rows(True)
        pad_rows(False)

    slot = i & 1
    stage = stage_ref.at[slot]
    h = h_ref[...]
    chunk = MXU_WIDTH
    lp = [lpos_ref[k:k + 1, :] for k in range(TOP_K)]
    r = lax.broadcasted_iota(jnp.int32, (chunk, tm), 0).astype(F32).astype(BF16)
    one, zero = jnp.ones((chunk, tm), BF16), jnp.zeros((chunk, tm), BF16)
    for c in range(stage.shape[0] // chunk):
        hit = None
        for k in range(TOP_K):
            u = lp[k] - c * chunk
            u = jnp.where(jnp.logical_and(u >= 0, u < chunk), u, -1).astype(F32).astype(BF16)
            hit = (r == u) if hit is None else (hit | (r == u))
        perm = jnp.where(hit, one, zero)
        rows = jnp.dot(perm, h, preferred_element_type=F32)
        stage[c * chunk:(c + 1) * chunk, :] = _pack_bf16_pairs(rows)

    def segments(tile, tile_slot, start):
        def per_segment(e, carry):
            s = tile * N_EXPERTS + e
            _segment_copies(stage_ref.at[tile_slot], ssrc_ref[s], xs_ref, sdst_ref[s], slen_ref[s], tm,
                            sem.at[tile_slot], start)
            return carry
        lax.fori_loop(0, N_EXPERTS, per_segment, 0)

    segments(i, slot, True)

    @pl.when(i > 0)
    def _():
        segments(i - 1, 1 - slot, False)

    @pl.when(i == pl.num_programs(0) - 1)
    def _():
        segments(i, slot, False)


def _dispatch(counts, pad_start, padded, seg_src, seg_len, seg_dst, lpos, h2, cap):
    n, d = h2.shape
    tm = min(TOKEN_TILE, n)
    return pl.pallas_call(
        _dispatch_kernel,
        grid_spec=pltpu.PrefetchScalarGridSpec(
            num_scalar_prefetch=6,
            grid=(n // tm,),
            in_specs=[
                pl.BlockSpec((TOP_K, tm), lambda i, *_: (0, i)),
                pl.BlockSpec((tm, d), lambda i, *_: (i, 0)),
            ],
            out_specs=pl.BlockSpec(memory_space=pl.ANY),
            scratch_shapes=[pltpu.VMEM((2, _stage_rows(tm), d // 2), jnp.uint32),
                            pltpu.VMEM((MOE_BLOCK, d // 2), jnp.uint32),
                            pltpu.SemaphoreType.DMA((2,)), pltpu.SemaphoreType.DMA],
        ),
        out_shape=jax.ShapeDtypeStruct((cap, d // 2), jnp.uint32),
        compiler_params=pltpu.CompilerParams(
            dimension_semantics=("arbitrary",), vmem_limit_bytes=VMEM_LIMIT),
        name="dispatch",
    )(counts, pad_start, padded, seg_src, seg_len, seg_dst, lpos, h2)


def _expert_kernel(pstart_ref, padded_ref, nact_ref, xs_ref, w1_ref, b1_ref, w2_ref, b2_ref, y_ref,
                   w1p_ref, w2b_ref, xbuf_ref, ybuf_ref, xsem, ysem):
    e = pl.program_id(0)
    blk = xbuf_ref.shape[1]
    n_active = nact_ref[0]
    first = pstart_ref[e] // blk
    count = padded_ref[e] // blk
    gw = MXU_WIDTH
    half = gw // 2

    def x_copy(g, slot):
        return pltpu.make_async_copy(xs_ref.at[pl.ds(pl.multiple_of(g * blk, blk), blk), :],
                                     xbuf_ref.at[slot], xsem.at[slot])

    def y_copy(g, slot):
        return pltpu.make_async_copy(ybuf_ref.at[slot],
                                     y_ref.at[pl.ds(pl.multiple_of(g * blk, blk), blk), :], ysem.at[slot])

    @pl.when(jnp.logical_and(e == 0, n_active > 0))
    def _():
        x_copy(0, 0).start()

    @pl.when(count > 0)
    def _():
        r = lax.broadcasted_iota(jnp.int32, (gw, gw), 0)
        c = lax.broadcasted_iota(jnp.int32, (gw, gw), 1)
        perm = jnp.where(r == jnp.where(c < half, 2 * c, 2 * (c - half) + 1), 1.0, 0.0).astype(BF16)
        for j in range(w1_ref.shape[1] // gw):
            w = w1_ref[:, j * gw:(j + 1) * gw].astype(BF16)
            w1p_ref[:, j * gw:(j + 1) * gw] = jnp.dot(w, perm, preferred_element_type=F32).astype(BF16)
        w2b_ref[...] = w2_ref[...].astype(BF16)

    def block(i, carry):
        g = first + i
        slot = g & 1
        x_copy(g, slot).wait()

        @pl.when(g + 1 < n_active)
        def _():
            x_copy(g + 1, 1 - slot).start()

        @pl.when(g >= 2)
        def _():
            y_copy(g - 2, slot).wait()

        x = _unpack_bf16_pairs(xbuf_ref[slot])
        a = jnp.dot(x, w1p_ref[...], preferred_element_type=F32) + b1_ref[...]
        acts = []
        for j in range(a.shape[1] // gw):
            ag = jnp.minimum(a[:, j * gw:j * gw + half], SWIGLU_LIMIT)
            al = jnp.clip(a[:, j * gw + half:(j + 1) * gw], -SWIGLU_LIMIT, SWIGLU_LIMIT)
            acts.append((ag * (1.0 / (1.0 + jnp.exp(-SWIGLU_ALPHA * ag))) * (al + 1.0)).astype(BF16))
        act = jnp.concatenate(acts, axis=1)
        y = jnp.dot(act, w2b_ref[...], preferred_element_type=F32) + b2_ref[...]
        ybuf_ref[slot] = _pack_bf16_pairs(y)
        y_copy(g, slot).start()
        return carry

    lax.fori_loop(0, count, block, 0)

    @pl.when(e == pl.num_programs(0) - 1)
    def _():
        for back in (2, 1):
            @pl.when(n_active >= back)
            def _(back=back):
                y_copy(n_active - back, (n_active - back) & 1).wait()

        xbuf_ref[0] = _packed_zeros(xbuf_ref.shape[1:])
        n_tail = y_ref.shape[0] // blk - n_active

        def tail(start):
            def body(i, carry):
                cp = pltpu.make_async_copy(
                    xbuf_ref.at[0], y_ref.at[pl.ds(pl.multiple_of((n_active + i) * blk, blk), blk), :], xsem.at[0])
                cp.start() if start else cp.wait()
                return carry
            lax.fori_loop(0, n_tail, body, 0)

        tail(True)
        tail(False)


def _experts(pad_start, padded, nact, xs, w1, b1p, w2, b2):
    cap, dh = xs.shape
    d = 2 * dh
    de2 = w1.shape[2]
    de = w2.shape[1]
    exp3 = lambda e, *_: (e, 0, 0)
    return pl.pallas_call(
        _expert_kernel,
        grid_spec=pltpu.PrefetchScalarGridSpec(
            num_scalar_prefetch=3,
            grid=(N_EXPERTS,),
            in_specs=[
                pl.BlockSpec(memory_space=pl.ANY),
                pl.BlockSpec((None, d, de2), exp3),
                pl.BlockSpec((None, 1, de2), exp3),
                pl.BlockSpec((None, de, d), exp3),
                pl.BlockSpec((None, 1, d), exp3),
            ],
            out_specs=pl.BlockSpec(memory_space=pl.ANY),
            scratch_shapes=[pltpu.VMEM((d, de2), BF16), pltpu.VMEM((de, d), BF16),
                            pltpu.VMEM((2, MOE_BLOCK, dh), jnp.uint32), pltpu.VMEM((2, MOE_BLOCK, dh), jnp.uint32),
                            pltpu.SemaphoreType.DMA((2,)), pltpu.SemaphoreType.DMA((2,))],
        ),
        out_shape=jax.ShapeDtypeStruct((cap, dh), jnp.uint32),
        compiler_params=pltpu.CompilerParams(
            dimension_semantics=("arbitrary",), vmem_limit_bytes=VMEM_LIMIT),
        name="experts",
    )(pad_start, padded, nact, xs, w1, b1p, w2, b2)


def _combine_kernel(ssrc_ref, slen_ref, sdst_ref, lpos_ref, gate_ref, y_ref, x1_ref, fg_ref, o_ref,
                    stage_ref, g_ref, sem):
    i = pl.program_id(0)
    tm = x1_ref.shape[0]

    slot = i & 1

    def segments(tile, tile_slot, start):
        def per_segment(e, carry):
            s = tile * N_EXPERTS + e
            _segment_copies(y_ref, sdst_ref[s], stage_ref.at[tile_slot], ssrc_ref[s], slen_ref[s], tm,
                            sem.at[tile_slot], start)
            return carry
        lax.fori_loop(0, N_EXPERTS, per_segment, 0)

    @pl.when(i == 0)
    def _():
        for n in range(stage_ref.shape[0]):
            stage_ref[n] = _packed_zeros(stage_ref.shape[1:])
        segments(0, 0, True)

    @pl.when(i + 1 < pl.num_programs(0))
    def _():
        segments(i + 1, 1 - slot, True)

    segments(i, slot, False)

    chunk = MXU_WIDTH
    lp = [lpos_ref[:, k:k + 1] for k in range(TOP_K)]
    gt = [gate_ref[:, k:k + 1] for k in range(TOP_K)]
    for c in range(stage_ref.shape[1] // chunk):
        r = lax.broadcasted_iota(jnp.int32, (tm, chunk), 1) + c * chunk
        g = jnp.zeros((tm, chunk), F32)
        for k in range(TOP_K):
            g = jnp.where(r == lp[k], gt[k], g)
        g_ref[:, c * chunk:(c + 1) * chunk] = g.astype(BF16)

    y = _unpack_bf16_pairs(stage_ref[slot])
    x = x1_ref[...] + jnp.dot(g_ref[...], y, preferred_element_type=F32)
    o_ref[...] = _rms(x, fg_ref[...])


def _combine(seg_src, seg_len, seg_dst, lpos_col, gate_col, y, x1, fg):
    n, d = x1.shape
    tm = min(TOKEN_TILE, n)
    return pl.pallas_call(
        _combine_kernel,
        grid_spec=pltpu.PrefetchScalarGridSpec(
            num_scalar_prefetch=3,
            grid=(n // tm,),
            in_specs=[
                pl.BlockSpec((tm, TOP_K), lambda i, *_: (i, 0)),
                pl.BlockSpec((tm, TOP_K), lambda i, *_: (i, 0)),
                pl.BlockSpec(memory_space=pl.ANY),
                pl.BlockSpec((tm, d), lambda i, *_: (i, 0)),
                pl.BlockSpec((1, d), lambda i, *_: (0, 0)),
            ],
            out_specs=pl.BlockSpec((tm, d), lambda i, *_: (i, 0)),
            scratch_shapes=[pltpu.VMEM((2, _stage_rows(tm), d // 2), jnp.uint32),
                            pltpu.VMEM((tm, _stage_rows(tm)), BF16),
                            pltpu.SemaphoreType.DMA((2,))],
        ),
        out_shape=jax.ShapeDtypeStruct((n, d), F32),
        compiler_params=pltpu.CompilerParams(
            dimension_semantics=("arbitrary",), vmem_limit_bytes=VMEM_LIMIT),
        name="combine",
    )(seg_src, seg_len, seg_dst, lpos_col, gate_col, y, x1, fg)


def _layer(x2, batch, seq, layer, lam_init, rel_bias_table, mix_norm_g, w_in, sgu_ln_g, sgu_ln_b, sgu_w,
           sgu_b, gmlp_out_g, lambda_q1, lambda_k1, lambda_q2, lambda_k2, diff_subln_g, w_out,
           ffn_norm_g, router_w, router_b, w_mlp1, b_mlp1, w_mlp2, b_mlp2):
    n, d = x2.shape
    l = layer
    row = lambda a: a[l].reshape(1, -1).astype(F32)

    gout, qt, k, vt = _mix_in(x2, row(mix_norm_g), w_in[l].astype(BF16), row(sgu_ln_g), row(sgu_ln_b),
                              sgu_w[l].astype(F32), sgu_b[l].astype(F32)[:, :, None], row(gmlp_out_g))
    dout = _diff_attn(qt, k, vt, _near_bias_tiles(rel_bias_table),
                      (row(lambda_q1), row(lambda_k1), row(lambda_q2), row(lambda_k2)),
                      row(diff_subln_g), batch, seq, lam_init)
    x1, h2, gate, lpos, lstart, lcnt, gbase, counts = _mix_out(
        x2, gout, dout, w_out[l].astype(BF16), row(ffn_norm_g),
        router_w[l].astype(F32).T, router_b[l].astype(F32)[:, None])

    counts = counts[:, 0]
    padded = (counts + MOE_BLOCK - 1) // MOE_BLOCK * MOE_BLOCK
    pad_end = jnp.cumsum(padded)
    pad_start = pad_end - padded
    n_tiles = lstart.shape[0] // N_EXPERTS
    nblk = pl.cdiv(n * TOP_K + n_tiles * N_EXPERTS * (SUBLANES - 1), MOE_BLOCK) + N_EXPERTS
    cap = nblk * MOE_BLOCK
    n_active = (pad_end[-1] // MOE_BLOCK).astype(jnp.int32).reshape(1)
    seg_src = lstart[:, 0]
    seg_len = lcnt[:, 0]
    seg_dst = gbase[:, 0] + jnp.tile(pad_start, n_tiles)

    xs = _dispatch(counts, pad_start, padded, seg_src, seg_len, seg_dst, lpos, h2, cap)
    half = MXU_WIDTH // 2
    b1p = b_mlp1[l].astype(F32).reshape(N_EXPERTS, -1, half, 2).transpose(0, 1, 3, 2).reshape(N_EXPERTS, 1, -1)
    y = _experts(pad_start, padded, n_active, xs, w_mlp1[l].astype(F32), b1p,
                 w_mlp2[l].astype(F32), b_mlp2[l][:, None, :].astype(F32))
    return (seg_src, seg_len, seg_dst, lpos.T, gate.T), y, x1


def kernel(x, rel_bias_table, mix_norm_g, w_in, sgu_ln_g, sgu_ln_b, sgu_w, sgu_b, gmlp_out_g, lambda_q1,
           lambda_k1, lambda_q2, lambda_k2, diff_subln_g, w_out, ffn_norm_g, router_w, router_b, w_mlp1,
           b_mlp1, w_mlp2, b_mlp2, final_norm_g):
    batch, seq, d = x.shape
    depth = w_in.shape[0]
    assert depth == 1, "the final RMSNorm is fused into the last layer's combine step"
    assert seq % (2 * ATT_TILE) == 0 and ATT_TILE == TOKEN_TILE
    x2 = x.reshape(batch * seq, d).astype(F32)
    lam_init = 0.8 - 0.6 * math.exp(-0.3 * 0)
    routing, y, x1 = _layer(
        x2, batch, seq, 0, lam_init, rel_bias_table, mix_norm_g, w_in, sgu_ln_g, sgu_ln_b, sgu_w, sgu_b,
        gmlp_out_g, lambda_q1, lambda_k1, lambda_q2, lambda_k2, diff_subln_g, w_out, ffn_norm_g,
        router_w, router_b, w_mlp1, b_mlp1, w_mlp2, b_mlp2)
    out = _combine(*routing, y, x1, final_norm_g.reshape(1, d).astype(F32))
    return out.reshape(batch, seq, d).astype(x.dtype)
```
